```python
import math
import numpy as np
import jax
import jax.numpy as jnp
from jax import lax

D_MODEL = 2048
BATCH = 4
SEQ = 2048
DEPTH = 1
DEC_BATCH = 32
DEC_SEQ = 1
PAST_LEN = 8192
PAGE_SIZE = 128

HEAD_DIM = 128
NSA_HEADS = 8
NSA_KV = 2
NSA_REP = NSA_HEADS // NSA_KV
CMP_STRIDE = 16
CMP_LEN = 2 * CMP_STRIDE
SLC_LEN = 64
SLC_TOPN = 16
WINDOW = 512
SLC_QBLOCK = 64
WIN_QBLOCK = 128
DN_HEADS = 8
DN_DK = 128
DN_DV = 128
DN_QKV = DN_HEADS * (2 * DN_DK + DN_DV)
DN_CONV = 4
DN_CHUNK = 64
PEER_HEADS = 8
PEER_NKEYS = 128
PEER_EXPERTS = PEER_NKEYS * PEER_NKEYS
PEER_DKEY = 256
PEER_TOPK = 16
PEER_TBLOCK = 128
ROPE_THETA = 10000.0
EPS = 1e-6
NEG = -1e30
IN_SPLITS = (NSA_HEADS * HEAD_DIM, 6 * NSA_KV * HEAD_DIM, NSA_HEADS * 3, DN_QKV, DN_HEADS, DN_HEADS, DN_HEADS * DN_DV, 2 * D_MODEL)
IN_WIDTH = sum(IN_SPLITS)
F32 = jnp.float32

kernel_name = 'nsa_gdn_peer_hybrid_step'


def _split_cols(a, sizes):
    return jnp.split(a, [int(s) for s in np.cumsum(sizes)[:-1]], axis=-1)


def _rmsnorm(x, w):
    xf = x.astype(F32)
    return xf * lax.rsqrt(jnp.mean(xf * xf, axis=-1, keepdims=True) + EPS) * w.astype(F32)


def _l2norm(x):
    xf = x.astype(F32)
    return xf * lax.rsqrt(jnp.sum(xf * xf, axis=-1, keepdims=True) + EPS)


def _rope(x, pos):
    half = HEAD_DIM // 2
    inv = ROPE_THETA ** (-jnp.arange(half, dtype=F32) / half)
    ang = jnp.asarray(pos, F32)[:, None] * inv
    cos = jnp.cos(ang)[:, None, :]
    sin = jnp.sin(ang)[:, None, :]
    x1 = x[..., :half].astype(F32)
    x2 = x[..., half:].astype(F32)
    return jnp.concatenate([x1 * cos - x2 * sin, x1 * sin + x2 * cos], axis=-1)


def _compress(rows, w, b):
    B, L = rows.shape[:2]
    n_cmp = (L - CMP_LEN) // CMP_STRIDE + 1
    n_chunk = -(-L // CMP_STRIDE)
    rows = jnp.pad(rows, ((0, 0), (0, n_chunk * CMP_STRIDE - L), (0, 0), (0, 0)))
    ch = rows.reshape(B, n_chunk, CMP_STRIDE, NSA_KV, HEAD_DIM)
    first = jnp.einsum('bnjgd,jd->bngd', ch, w[:CMP_STRIDE])
    second = jnp.einsum('bnjgd,jd->bngd', ch, w[CMP_STRIDE:])
    return first[:, :n_cmp] + second[:, 1:n_cmp + 1] + b


def _overlap(n_cmp, n_slc):
    cs = np.arange(n_cmp)[:, None] * CMP_STRIDE
    ss = np.arange(n_slc)[None, :] * SLC_LEN
    return ((cs < ss + SLC_LEN) & (cs + CMP_LEN > ss)).astype(np.float32)


def _nsa(q, kc, vc, ks, vs, kw, vw, gate, q_off, w_off, P):
    B, Tq = q.shape[:2]
    L = kc.shape[1]
    scale = HEAD_DIM ** -0.5
    qpos = q_off + np.arange(Tq)
    qg = q.astype(F32).reshape(B, Tq, NSA_KV, NSA_REP, HEAD_DIM)

    n_cmp = (L - CMP_LEN) // CMP_STRIDE + 1
    cpos = np.arange(n_cmp) * CMP_STRIDE + CMP_LEN - 1
    ck = _rope(_compress(kc, P['w_cmp_k'], P['b_cmp_k']), cpos)
    cv = _compress(vc, P['w_cmp_v'], P['b_cmp_v']).astype(F32)
    cmask = cpos[None, :] <= qpos[:, None]
    s = jnp.einsum('bqgrd,bngd->bgrqn', qg, ck) * scale
    p_cmp = jax.nn.softmax(jnp.where(cmask, s, NEG), axis=-1) * cmask
    o_cmp = jnp.einsum('bgrqn,bngd->bqgrd', p_cmp, cv)

    n_slc = -(-L // SLC_LEN)
    n_sel = min(SLC_TOPN, n_slc)
    imp = jnp.einsum('bgrqn,ns->bgqs', p_cmp, _overlap(n_cmp, n_slc))
    blk = np.arange(n_slc)[None, :]
    cur = (qpos // SLC_LEN)[:, None]
    visible = blk <= cur
    forced = (blk == 0) | (blk == cur) | (blk == cur - 1)
    score = jnp.where(forced, 1e9, jnp.where(visible, imp, -1e9))
    top_s, idx = lax.top_k(score, n_sel)
    valid = top_s > -1e8

    pad_l = n_slc * SLC_LEN - L

    def blocks(a):
        a = jnp.pad(a.astype(F32), ((0, 0), (0, pad_l), (0, 0), (0, 0)))
        return a.reshape(B, n_slc, SLC_LEN, NSA_KV, HEAD_DIM).transpose(0, 3, 1, 2, 4)

    ksb, vsb = blocks(ks), blocks(vs)
    qb_len = min(SLC_QBLOCK, Tq)
    nqb = -(-Tq // qb_len)
    pq = nqb * qb_len - Tq
    q_blk = jnp.pad(qg, ((0, 0), (0, pq), (0, 0), (0, 0), (0, 0))).reshape(
        B, nqb, qb_len, NSA_KV, NSA_REP, HEAD_DIM).transpose(1, 0, 2, 3, 4, 5)

    def per_qblock(a, fill):
        a = jnp.pad(a, ((0, 0), (0, 0), (0, pq), (0, 0)), constant_values=fill)
        return a.reshape(B, NSA_KV, nqb, qb_len, n_sel).transpose(2, 0, 1, 3, 4)

    idx_blk = per_qblock(idx, 0)
    val_blk = per_qblock(valid, False)
    pos_blk = jnp.asarray(np.pad(qpos, (0, pq), mode='edge').reshape(nqb, qb_len), jnp.int32)
    gather = jax.vmap(jax.vmap(lambda tab, ix: tab[ix]))

    def sel_attend(args):
        qx, ix, vl, ps = args
        kg = gather(ksb, ix)
        vg = gather(vsb, ix)
        kpos = ix[..., None] * SLC_LEN + jnp.arange(SLC_LEN)
        mask = vl[..., None] & (kpos <= ps[None, None, :, None, None])
        sc = jnp.einsum('bqgrd,bgqnsd->bgrqns', qx, kg) * scale
        sc = jnp.where(mask[:, :, None], sc, NEG)
        shp = sc.shape
        pr = jax.nn.softmax(sc.reshape(shp[:4] + (-1,)), axis=-1).reshape(shp)
        return jnp.einsum('bgrqns,bgqnsd->bqgrd', pr, vg)

    o_slc = lax.map(sel_attend, (q_blk, idx_blk, val_blk, pos_blk))
    o_slc = o_slc.transpose(1, 0, 2, 3, 4, 5).reshape(B, nqb * qb_len, NSA_KV, NSA_REP, HEAD_DIM)[:, :Tq]

    wq_len = min(WIN_QBLOCK, Tq)
    nwb = -(-Tq // wq_len)
    pw = nwb * wq_len - Tq
    span = WINDOW + wq_len
    widx = (q_off - w_off) + np.arange(nwb)[:, None] * wq_len + np.arange(span)[None, :]
    kpos = widx - WINDOW + w_off
    qposw = (q_off + np.arange(nwb * wq_len)).reshape(nwb, wq_len)
    wmask = ((kpos[:, None, :] <= qposw[:, :, None]) & (kpos[:, None, :] >= qposw[:, :, None] - WINDOW)
             & (kpos[:, None, :] >= w_off))

    def band(a):
        return jnp.pad(a.astype(F32), ((0, 0), (WINDOW, pw), (0, 0), (0, 0)))[:, widx]

    kwb, vwb = band(kw), band(vw)
    q_w = jnp.pad(qg, ((0, 0), (0, pw), (0, 0), (0, 0), (0, 0))).reshape(
        B, nwb, wq_len, NSA_KV, NSA_REP, HEAD_DIM)
    s = jnp.einsum('bnqgrd,bnkgd->bngrqk', q_w, kwb) * scale
    p = jax.nn.softmax(jnp.where(wmask[None, :, None, None], s, NEG), axis=-1)
    o_win = jnp.einsum('bngrqk,bnkgd->bnqgrd', p, vwb).reshape(
        B, nwb * wq_len, NSA_KV, NSA_REP, HEAD_DIM)[:, :Tq]

    g = jax.nn.sigmoid(gate.astype(F32)).reshape(B, Tq, NSA_KV, NSA_REP, 3)
    o = g[..., 0:1] * o_cmp + g[..., 1:2] * o_slc + g[..., 2:3] * o_win
    return o.reshape(B, Tq, NSA_HEADS * HEAD_DIM)


def _gated_delta(q, k, v, g, beta, S0):
    B, T, H, DK = q.shape
    DV = v.shape[-1]
    C = min(DN_CHUNK, T)
    n = -(-T // C)
    pad = n * C - T

    def chunks(a):
        a = jnp.pad(a.astype(F32), [(0, 0), (0, pad)] + [(0, 0)] * (a.ndim - 2))
        a = a.reshape((B, n, C) + a.shape[2:])
        return a.transpose((1, 0, 3, 2) + tuple(range(4, a.ndim)))

    qc = chunks(q) * DK ** -0.5
    kc, vc, gc, bc = chunks(k), chunks(v), chunks(g), chunks(beta)
    G = jnp.cumsum(gc, axis=-1)
    tril = np.tril(np.ones((C, C), bool))
    strict = np.tril(np.ones((C, C), bool), -1)
    diff = G[..., :, None] - G[..., None, :]
    decay = jnp.where(tril, jnp.exp(jnp.where(tril, diff, 0.0)), 0.0)
    kb = kc * bc[..., None]
    M = jnp.where(strict, jnp.einsum('nbhid,nbhjd->nbhij', kb, kc) * decay, 0.0)
    rhs = jnp.concatenate([vc * bc[..., None], kb * jnp.exp(G)[..., None]], axis=-1)
    sol = lax.linalg.triangular_solve(M + jnp.eye(C, dtype=F32), rhs, left_side=True, lower=True,
                                      unit_diagonal=True)
    u, w = sol[..., :DV], sol[..., DV:]
    a_qk = jnp.where(tril, jnp.einsum('nbhid,nbhjd->nbhij', qc, kc) * decay, 0.0)

    def step(S, inp):
        q_i, k_i, u_i, w_i, a_i, G_i = inp
        v_new = u_i - jnp.einsum('bhck,bhkv->bhcv', w_i, S)
        o_i = (jnp.einsum('bhck,bhkv->bhcv', q_i * jnp.exp(G_i)[..., None], S)
               + jnp.einsum('bhij,bhjv->bhiv', a_i, v_new))
        G_last = G_i[..., -1]
        S = (S * jnp.exp(G_last)[..., None, None]
             + jnp.einsum('bhck,bhcv->bhkv', k_i * jnp.exp(G_last[..., None] - G_i)[..., None], v_new))
        return S, o_i

    S, o = lax.scan(step, S0.astype(F32), (qc, kc, u, w, a_qk, G))
    o = o.transpose(1, 0, 3, 2, 4).reshape(B, n * C, H, DV)[:, :T]
    return o, S


def _deltanet(qkv, a, b, z, conv_hist, S0, P):
    B, T, _ = qkv.shape
    xx = jnp.concatenate([conv_hist.astype(F32), qkv.astype(F32)], axis=1)
    y = xx[:, 0:T] * P['dn_conv_w'][0]
    for j in range(1, DN_CONV):
        y = y + xx[:, j:j + T] * P['dn_conv_w'][j]
    y = jax.nn.silu(y)
    q, k, v = _split_cols(y, (DN_HEADS * DN_DK, DN_HEADS * DN_DK, DN_HEADS * DN_DV))
    q = _l2norm(q.reshape(B, T, DN_HEADS, DN_DK))
    k = _l2norm(k.reshape(B, T, DN_HEADS, DN_DK))
    v = v.reshape(B, T, DN_HEADS, DN_DV)
    beta = jax.nn.sigmoid(b.astype(F32))
    g = -jnp.exp(P['dn_A_log']) * jax.nn.softplus(a.astype(F32) + P['dn_dt_bias'])
    o, S = _gated_delta(q, k, v, g, beta, S0)
    o = _rmsnorm(o, P['dn_norm_w']) * jax.nn.silu(z.astype(F32).reshape(B, T, DN_HEADS, DN_DV))
    return o.reshape(B, T, DN_HEADS * DN_DV), xx[:, -(DN_CONV - 1):], S


def _peer(h, P):
    B, T, D = h.shape
    N = B * T
    TB = min(PEER_TBLOCK, N)
    nb = -(-N // TB)
    hb = jnp.pad(h.reshape(N, D), ((0, nb * TB - N), (0, 0))).reshape(nb, TB, D)
    half = PEER_DKEY // 2

    def block(xb):
        qh = jnp.einsum('td,hdk->thk', xb, P['w_peer_q'])
        s1 = jnp.einsum('thk,hnk->thn', qh[..., :half], P['w_peer_keys'][:, 0])
        s2 = jnp.einsum('thk,hnk->thn', qh[..., half:], P['w_peer_keys'][:, 1])
        v1, i1 = lax.top_k(s1, PEER_TOPK)
        v2, i2 = lax.top_k(s2, PEER_TOPK)
        cand = (v1[..., :, None] + v2[..., None, :]).reshape(TB, PEER_HEADS, PEER_TOPK * PEER_TOPK)
        cidx = (i1[..., :, None] * PEER_NKEYS + i2[..., None, :]).reshape(TB, PEER_HEADS, PEER_TOPK * PEER_TOPK)
        best, sel = lax.top_k(cand, PEER_TOPK)
        eidx = jnp.take_along_axis(cidx, sel, axis=-1)
        wts = jax.nn.softmax(best.astype(F32), axis=-1)
        act = jax.nn.gelu(jnp.einsum('thkd,td->thk', P['w_peer_u'][eidx], xb))
        return jnp.einsum('thk,thkd->td', wts * act, P['w_peer_v'][eidx])

    return lax.map(block, hb).reshape(nb * TB, D)[:N].reshape(B, T, D)


def _layer(x, c, past, P):
    B, T, _ = x.shape
    q_off = 0 if past is None else past[0].shape[1]
    pos = q_off + np.arange(T)
    ada = jnp.einsum('bd,de->be', c, P['w_ada']) + P['b_ada']
    sh1, sc1, gt1, sh2, sc2, gt2 = jnp.split(ada[:, None, :].astype(F32), 6, axis=-1)

    h = _rmsnorm(x, P['norm1_w']) * (1.0 + sc1) + sh1
    proj = jnp.einsum('btd,de->bte', h, P['w_in'])
    q, kv, nsa_gate, dn_qkv, dn_a, dn_b, dn_z, merge = _split_cols(proj, IN_SPLITS)
    kc, vc, ksl, vsl, kw, vw = [a.reshape(B, T, NSA_KV, HEAD_DIM) for a in jnp.split(kv, 6, axis=-1)]
    q = _rope(_rmsnorm(q.reshape(B, T, NSA_HEADS, HEAD_DIM), P['q_norm_w']), pos)
    kc = _rmsnorm(kc, P['k_norm_w'][0])
    ksl = _rope(_rmsnorm(ksl, P['k_norm_w'][1]), pos)
    kw = _rope(_rmsnorm(kw, P['k_norm_w'][2]), pos)
    if past is None:
        kc_all, vc_all, ks_all, vs_all, kw_all, vw_all = kc, vc, ksl, vsl, kw, vw
        win_keep = min(WINDOW, T)
        w_off = 0
        conv_hist = jnp.zeros((B, DN_CONV - 1, DN_QKV), F32)
        S0 = jnp.zeros((B, DN_HEADS, DN_DK, DN_DV), F32)
    else:
        pkc, pvc, pks, pvs, bwk, bwv, conv_hist, S0 = past

        def cat(a, b):
            return jnp.concatenate([a.astype(F32), b.astype(F32)], axis=1)

        kc_all, vc_all, ks_all, vs_all = cat(pkc, kc), cat(pvc, vc), cat(pks, ksl), cat(pvs, vsl)
        kw_all, vw_all = cat(bwk, kw), cat(bwv, vw)
        win_keep = bwk.shape[1]
        w_off = q_off - win_keep
    o_nsa = _nsa(q, kc_all, vc_all, ks_all, vs_all, kw_all, vw_all, nsa_gate, q_off, w_off, P)
    o_dn, conv_new, S_new = _deltanet(dn_qkv, dn_a, dn_b, dn_z, conv_hist, S0, P)
    g_nsa, g_dn = jnp.split(jax.nn.sigmoid(merge.astype(F32)), 2, axis=-1)
    mixed = (g_nsa * jnp.einsum('bte,ed->btd', o_nsa, P['w_br_a'])
             + g_dn * jnp.einsum('bte,ed->btd', o_dn, P['w_br_b']))
    x = x + gt1 * jnp.einsum('btd,de->bte', mixed, P['w_out'])

    h2 = _rmsnorm(x, P['norm2_w']) * (1.0 + sc2) + sh2
    x = x + gt2 * _peer(h2, P)
    return (x, kc, vc, ksl, vsl, kw_all[:, -win_keep:], vw_all[:, -win_keep:], conv_new, S_new)


def setup_inputs(seed: int = 0) -> dict:
    key = jax.random.key(seed)
    keys = iter(jax.random.split(key, 64))

    def nrm(shape, scale=1.0):
        return jax.random.normal(next(keys), shape, F32) * scale

    def gain(shape):
        return 1.0 + nrm(shape, 0.02)

    n_pages = PAST_LEN // PAGE_SIZE
    n_used = DEC_BATCH * n_pages
    n_pool = n_used + -(-n_used // 4)
    page_table = jax.random.permutation(next(keys), n_pool)[:n_used].reshape(DEC_BATCH, n_pages).astype(jnp.int32)
    win_buf = min(WINDOW, PAST_LEN)
    pool = (DEPTH, n_pool, PAGE_SIZE, NSA_KV, HEAD_DIM)
    dt = jnp.exp(jax.random.uniform(next(keys), (DEPTH, DN_HEADS), F32, math.log(1e-3), math.log(1e-1)))
    a_log = jnp.log(jax.random.uniform(next(keys), (DEPTH, DN_HEADS), F32, 1.0, 16.0))
    return {
        'x_prompt': nrm((BATCH, SEQ, D_MODEL)),
        'x_sample': nrm((DEC_BATCH, DEC_SEQ, D_MODEL)),
        'cache_cmp_k': nrm(pool),
        'cache_cmp_v': nrm(pool),
        'cache_slc_k': nrm(pool),
        'cache_slc_v': nrm(pool),
        'state_win_k': nrm((DEPTH, DEC_BATCH, win_buf, NSA_KV, HEAD_DIM)),
        'state_win_v': nrm((DEPTH, DEC_BATCH, win_buf, NSA_KV, HEAD_DIM)),
        'state_conv': nrm((DEPTH, DEC_BATCH, DN_CONV - 1, DN_QKV)),
        'state_delta': nrm((DEPTH, DEC_BATCH, DN_HEADS, DN_DK, DN_DV), 0.05),
        'page_table': page_table,
        'c_prompt': nrm((BATCH, D_MODEL)),
        'c_sample': nrm((DEC_BATCH, D_MODEL)),
        'w_ada': nrm((DEPTH, D_MODEL, 6 * D_MODEL), 0.5 * D_MODEL ** -0.5),
        'b_ada': nrm((DEPTH, 6 * D_MODEL), 0.02),
        'norm1_w': gain((DEPTH, D_MODEL)),
        'norm2_w': gain((DEPTH, D_MODEL)),
        'w_in': nrm((DEPTH, D_MODEL, IN_WIDTH), D_MODEL ** -0.5),
        'q_norm_w': gain((DEPTH, HEAD_DIM)),
        'k_norm_w': gain((DEPTH, 3, HEAD_DIM)),
        'w_cmp_k': 1.0 / CMP_LEN + nrm((DEPTH, CMP_LEN, HEAD_DIM), CMP_LEN ** -0.5),
        'b_cmp_k': nrm((DEPTH, HEAD_DIM), 0.02),
        'w_cmp_v': 1.0 / CMP_LEN + nrm((DEPTH, CMP_LEN, HEAD_DIM), CMP_LEN ** -0.5),
        'b_cmp_v': nrm((DEPTH, HEAD_DIM), 0.02),
        'dn_conv_w': nrm((DEPTH, DN_CONV, DN_QKV), DN_CONV ** -0.5),
        'dn_A_log': a_log,
        'dn_dt_bias': dt + jnp.log(-jnp.expm1(-dt)),
        'dn_norm_w': gain((DEPTH, DN_DV)),
        'w_br_a': nrm((DEPTH, NSA_HEADS * HEAD_DIM, D_MODEL), (NSA_HEADS * HEAD_DIM) ** -0.5),
        'w_br_b': nrm((DEPTH, DN_HEADS * DN_DV, D_MODEL), (DN_HEADS * DN_DV) ** -0.5),
        'w_out': nrm((DEPTH, D_MODEL, D_MODEL), D_MODEL ** -0.5),
        'w_peer_q': nrm((DEPTH, PEER_HEADS, D_MODEL, PEER_DKEY), D_MODEL ** -0.5),
        'w_peer_keys': nrm((DEPTH, PEER_HEADS, 2, PEER_NKEYS, PEER_DKEY // 2), (PEER_DKEY // 2) ** -0.5),
        'w_peer_u': nrm((DEPTH, PEER_EXPERTS, D_MODEL), D_MODEL ** -0.5),
        'w_peer_v': nrm((DEPTH, PEER_EXPERTS, D_MODEL), PEER_HEADS ** -0.5),
    }


def reference(x_prompt, x_sample, cache_cmp_k, cache_cmp_v, cache_slc_k, cache_slc_v,
              state_win_k, state_win_v, state_conv, state_delta, page_table, c_prompt, c_sample,
              w_ada, b_ada, norm1_w, norm2_w, w_in, q_norm_w, k_norm_w, w_cmp_k, b_cmp_k, w_cmp_v, b_cmp_v,
              dn_conv_w, dn_A_log, dn_dt_bias, dn_norm_w, w_br_a, w_br_b, w_out,
              w_peer_q, w_peer_keys, w_peer_u, w_peer_v):
    n_pages = page_table.shape[1]

    def paged(cache):
        rows = cache[page_table]
        return rows.reshape((rows.shape[0], n_pages * rows.shape[2]) + rows.shape[3:])

    yp, ys = x_prompt, x_sample
    p_hist, s_hist = [], []
    for l in range(DEPTH):
        P = {'w_ada': w_ada[l], 'b_ada': b_ada[l], 'norm1_w': norm1_w[l], 'norm2_w': norm2_w[l],
             'w_in': w_in[l], 'q_norm_w': q_norm_w[l], 'k_norm_w': k_norm_w[l],
             'w_cmp_k': w_cmp_k[l], 'b_cmp_k': b_cmp_k[l], 'w_cmp_v': w_cmp_v[l], 'b_cmp_v': b_cmp_v[l],
             'dn_conv_w': dn_conv_w[l], 'dn_A_log': dn_A_log[l], 'dn_dt_bias': dn_dt_bias[l],
             'dn_norm_w': dn_norm_w[l], 'w_br_a': w_br_a[l], 'w_br_b': w_br_b[l], 'w_out': w_out[l],
             'w_peer_q': w_peer_q[l], 'w_peer_keys': w_peer_keys[l], 'w_peer_u': w_peer_u[l],
             'w_peer_v': w_peer_v[l]}
        past = (paged(cache_cmp_k[l]), paged(cache_cmp_v[l]), paged(cache_slc_k[l]), paged(cache_slc_v[l]),
                state_win_k[l], state_win_v[l], state_conv[l], state_delta[l])
        yp, *p_new = _layer(yp, c_prompt, None, P)
        ys, *s_new = _layer(ys, c_sample, past, P)
        p_hist.append(p_new)
        s_hist.append(s_new)
    p_cmp_k, p_cmp_v, p_slc_k, p_slc_v, p_win_k, p_win_v, p_conv, p_delta = [jnp.stack(a) for a in zip(*p_hist)]
    s_cmp_k, s_cmp_v, s_slc_k, s_slc_v, s_win_k, s_win_v, s_conv, s_delta = [jnp.stack(a) for a in zip(*s_hist)]
    return (yp, ys, p_cmp_k, p_cmp_v, p_slc_k, p_slc_v, p_win_k, p_win_v, p_conv, p_delta,
            s_cmp_k, s_cmp_v, s_slc_k, s_slc_v, s_win_k, s_win_v, s_conv, s_delta)
```

```python
import functools

import numpy as np
import jax
import jax.numpy as jnp
from jax import lax
from jax.experimental import pallas as pl
from jax.experimental.pallas import tpu as pltpu

F32 = jnp.float32
BF16 = jnp.bfloat16

HEAD_DIM = 128
NSA_HEADS = 8
NSA_KV = 2
NSA_REP = NSA_HEADS // NSA_KV
CMP_STRIDE = 16
CMP_LEN = 2 * CMP_STRIDE
SLC_LEN = 64
SLC_TOPN = 16
WINDOW = 512
DN_HEADS = 8
DN_DK = 128
DN_DV = 128
DN_CONV = 4
DN_CHUNK = 128
PEER_HEADS = 8
PEER_NKEYS = 128
PEER_DKEY = 256
PEER_TOPK = 16
ROPE_THETA = 10000.0
EPS = 1e-6
NEG = -1e30
PAGE_GROUP = 8
GATE_COL = 0
DNA_COL = 24
DNB_COL = 32
MIB = 2 ** 20


def _cparams(semantics, vmem_mib):
    return pltpu.CompilerParams(dimension_semantics=semantics, vmem_limit_bytes=vmem_mib * MIB)


def _dg(a, b, ca=1, cb=0):
    return lax.dot_general(a, b, (((ca,), (cb,)), ((), ())), preferred_element_type=F32)


def _dot1(a, b, ca=1, cb=0):
    return _dg(a.astype(BF16), b.astype(BF16), ca, cb)


def _split2(a):
    hi = a.astype(BF16)
    return hi, (a - hi.astype(F32)).astype(BF16)


def _split3(a):
    hi = a.astype(BF16)
    r = a - hi.astype(F32)
    mid = r.astype(BF16)
    return hi, mid, (r - mid.astype(F32)).astype(BF16)


def _dot3(a, b, ca=1, cb=0):
    ah, al = _split2(a)
    bh, bl = _split2(b)
    return _dg(ah, bh, ca, cb) + _dg(ah, bl, ca, cb) + _dg(al, bh, ca, cb)


def _dot_sel_rhs(a, sel):
    return sum(_dg(p, sel) for p in _split3(a))


def _dot_sel_lhs(sel, b):
    return sum(_dg(sel, p) for p in _split3(b))


def _sigmoid(x):
    return 1.0 / (1.0 + jnp.exp(-x))


def _silu(x):
    return x * _sigmoid(x)


def _softplus(x):
    return jnp.maximum(x, 0.0) + jnp.log(1.0 + jnp.exp(-jnp.abs(x)))


def _gelu_tanh(x):
    return 0.5 * x * (1.0 + jnp.tanh(0.7978845608028654 * (x + 0.044715 * (x * x * x))))


def _rms(x, w):
    return x * lax.rsqrt(jnp.mean(x * x, axis=-1, keepdims=True) + EPS) * w


def _l2(x):
    return x * lax.rsqrt(jnp.sum(x * x, axis=-1, keepdims=True) + EPS)


def _rope(x, cos, sin_signed):
    return x * cos + pltpu.roll(x, HEAD_DIM // 2, 1) * sin_signed


def _masked_softmax_rows(s, mask):
    s = jnp.where(mask, s, NEG)
    m = jnp.max(s, axis=-1, keepdims=True)
    e = jnp.where(mask, jnp.exp(s - m), 0.0)
    den = jnp.sum(e, axis=-1, keepdims=True)
    return e / jnp.where(den > 0.0, den, 1.0)


def _rank_desc(score, lane_idx, n):
    rank = jnp.zeros(score.shape, F32)
    for s2 in range(n):
        col = score[:, s2:s2 + 1]
        beats = (col > score) | ((col == score) & (lane_idx > s2))
        rank = rank + jnp.where(beats, 1.0, 0.0)
    return rank


def _mm_body(*refs, has_bias):
    if has_bias:
        a_ref, b_ref, bias_ref, o_ref = refs
    else:
        a_ref, b_ref, o_ref = refs
    acc = _dot1(a_ref[...], b_ref[...])
    if has_bias:
        acc = acc + bias_ref[...]
    o_ref[...] = acc.astype(o_ref.dtype)


def _matmul(a, b, *, tn, bias=None, out_dtype=F32, tm_max=1024, name="matmul"):
    M, K = a.shape
    N = b.shape[1]
    tm = min(M, tm_max)
    tn = min(tn, N)
    assert M % tm == 0 and N % tn == 0
    in_specs = [pl.BlockSpec((tm, K), lambda i, j: (i, 0)), pl.BlockSpec((K, tn), lambda i, j: (0, j))]
    args = [a, b]
    if bias is not None:
        in_specs.append(pl.BlockSpec((1, tn), lambda i, j: (0, j)))
        args.append(bias.reshape(1, N))
    return pl.pallas_call(
        functools.partial(_mm_body, has_bias=bias is not None),
        out_shape=jax.ShapeDtypeStruct((M, N), out_dtype),
        grid=(M // tm, N // tn),
        in_specs=in_specs,
        out_specs=pl.BlockSpec((tm, tn), lambda i, j: (i, j)),
        compiler_params=_cparams(("parallel", "arbitrary"), 48),
        name=name,
    )(*args)


def _mod_specs(mod, rows_per_seq, tm, D):
    if rows_per_seq == 1:
        return mod, pl.BlockSpec((tm, D), lambda i: (i, 0))
    assert rows_per_seq % tm == 0
    tps = rows_per_seq // tm
    return mod.reshape(mod.shape[0], 1, D), pl.BlockSpec((None, 1, D), lambda i: (i // tps, 0, 0))


def _normmod_body(x_ref, w_ref, sc_ref, sh_ref, o_ref):
    h = _rms(x_ref[...], w_ref[...])
    o_ref[...] = (h * (1.0 + sc_ref[...]) + sh_ref[...]).astype(o_ref.dtype)


def _normmod(x, w, sc, sh, rows_per_seq):
    M, D = x.shape
    tm = min(512, rows_per_seq) if rows_per_seq > 1 else M
    sc_a, spec = _mod_specs(sc, rows_per_seq, tm, D)
    sh_a, _ = _mod_specs(sh, rows_per_seq, tm, D)
    return pl.pallas_call(
        _normmod_body,
        out_shape=jax.ShapeDtypeStruct((M, D), BF16),
        grid=(M // tm,),
        in_specs=[pl.BlockSpec((tm, D), lambda i: (i, 0)), pl.BlockSpec((1, D), lambda i: (0, 0)), spec, spec],
        out_specs=pl.BlockSpec((tm, D), lambda i: (i, 0)),
        compiler_params=_cparams(("parallel",), 40),
        name="normmod",
    )(x, w.reshape(1, D), sc_a, sh_a)


def _merge_body(on_ref, od_ref, g1_ref, g2_ref, wa_ref, wb_ref, o_ref):
    a = _dot1(on_ref[...], wa_ref[...])
    b = _dot1(od_ref[...], wb_ref[...])
    o_ref[...] = (_sigmoid(g1_ref[...]) * a + _sigmoid(g2_ref[...]) * b).astype(o_ref.dtype)


def _merge(o_nsa, o_dn, mg, wa, wb):
    M, K = o_nsa.shape
    D = wa.shape[1]
    tm = min(M, 512)
    tn = 1024
    nj = D // tn
    return pl.pallas_call(
        _merge_body,
        out_shape=jax.ShapeDtypeStruct((M, D), BF16),
        grid=(M // tm, nj),
        in_specs=[pl.BlockSpec((tm, K), lambda i, j: (i, 0)), pl.BlockSpec((tm, K), lambda i, j: (i, 0)),
                  pl.BlockSpec((tm, tn), lambda i, j: (i, j)), pl.BlockSpec((tm, tn), lambda i, j: (i, j + nj)),
                  pl.BlockSpec((K, tn), lambda i, j: (0, j)), pl.BlockSpec((K, tn), lambda i, j: (0, j))],
        out_specs=pl.BlockSpec((tm, tn), lambda i, j: (i, j)),
        compiler_params=_cparams(("parallel", "arbitrary"), 40),
        name="merge",
    )(o_nsa, o_dn, mg, mg, wa, wb)


def _outproj_body(x_ref, mx_ref, w_ref, gt_ref, nw_ref, sc_ref, sh_ref, x1_o, h2_o):
    x1 = x_ref[...] + gt_ref[...] * _dot1(mx_ref[...], w_ref[...])
    x1_o[...] = x1
    h2_o[...] = (_rms(x1, nw_ref[...]) * (1.0 + sc_ref[...]) + sh_ref[...]).astype(h2_o.dtype)


def _outproj(x, mixed, w_out, gt1, norm2_w, sc2, sh2, rows_per_seq):
    M, D = x.shape
    tm = min(256, rows_per_seq) if rows_per_seq > 1 else M
    gt_a, spec = _mod_specs(gt1, rows_per_seq, tm, D)
    sc_a, _ = _mod_specs(sc2, rows_per_seq, tm, D)
    sh_a, _ = _mod_specs(sh2, rows_per_seq, tm, D)
    row = pl.BlockSpec((tm, D), lambda i: (i, 0))
    return pl.pallas_call(
        _outproj_body,
        out_shape=(jax.ShapeDtypeStruct((M, D), F32), jax.ShapeDtypeStruct((M, D), BF16)),
        grid=(M // tm,),
        in_specs=[row, row, pl.BlockSpec((D, D), lambda i: (0, 0)), spec,
                  pl.BlockSpec((1, D), lambda i: (0, 0)), spec, spec],
        out_specs=(row, row),
        compiler_params=_cparams(("parallel",), 48),
        name="outproj",
    )(x, mixed, w_out, gt_a, norm2_w.reshape(1, D), sc_a, sh_a)


def _residual_body(x_ref, p_ref, gt_ref, o_ref):
    o_ref[...] = x_ref[...] + gt_ref[...] * p_ref[...]


def _residual(x1, peer, gt2, rows_per_seq):
    M, D = x1.shape
    tm = min(512, rows_per_seq) if rows_per_seq > 1 else M
    gt_a, spec = _mod_specs(gt2, rows_per_seq, tm, D)
    row = pl.BlockSpec((tm, D), lambda i: (i, 0))
    return pl.pallas_call(
        _residual_body,
        out_shape=jax.ShapeDtypeStruct((M, D), F32),
        grid=(M // tm,),
        in_specs=[row, row, spec],
        out_specs=row,
        compiler_params=_cparams(("parallel",), 40),
        name="residual",
    )(x1, peer, gt_a)


def _qkprep_body(qkv_ref, cos_ref, sin_ref, qw_ref, kw_ref,
                 q_o, kc_o, vc_o, ks_o, vs_o, kwn_o, vw_o, ksb_o, vsb_o, kwb_o, vwb_o):
    cos = cos_ref[...]
    sin = sin_ref[...]
    for h in range(NSA_HEADS):
        sl = slice(h * HEAD_DIM, (h + 1) * HEAD_DIM)
        q_o[:, sl] = _rope(_rms(qkv_ref[:, sl], qw_ref[...]), cos, sin)
    base = NSA_HEADS * HEAD_DIM
    gw = NSA_KV * HEAD_DIM
    for g in range(NSA_KV):
        sl = slice(g * HEAD_DIM, (g + 1) * HEAD_DIM)

        def col(part):
            return qkv_ref[:, base + part * gw + g * HEAD_DIM: base + part * gw + (g + 1) * HEAD_DIM]

        kc_o[:, sl] = _rms(col(0), kw_ref[0:1, :])
        vc_o[:, sl] = col(1)
        ks = _rope(_rms(col(2), kw_ref[1:2, :]), cos, sin)
        ks_o[:, sl] = ks
        ksb_o[:, sl] = ks.astype(BF16)
        vs = col(3)
        vs_o[:, sl] = vs
        vsb_o[:, sl] = vs.astype(BF16)
        kw = _rope(_rms(col(4), kw_ref[2:3, :]), cos, sin)
        kwn_o[:, sl] = kw
        kwb_o[:, sl] = kw.astype(BF16)
        vw = col(5)
        vw_o[:, sl] = vw
        vwb_o[:, sl] = vw.astype(BF16)


def _qkprep(qkv, cos, sin, q_norm_w, k_norm_w, rows_per_seq):
    M, W = qkv.shape
    tm = min(256, rows_per_seq) if rows_per_seq > 1 else M
    tps = max(rows_per_seq // tm, 1)
    gw = NSA_KV * HEAD_DIM
    row = lambda w: pl.BlockSpec((tm, w), lambda i: (i, 0))
    tab = pl.BlockSpec((tm, HEAD_DIM), lambda i: (i % tps, 0))
    shapes = [jax.ShapeDtypeStruct((M, NSA_HEADS * HEAD_DIM), F32)] + [jax.ShapeDtypeStruct((M, gw), F32)] * 6 \
        + [jax.ShapeDtypeStruct((M, gw), BF16)] * 4
    return pl.pallas_call(
        _qkprep_body,
        out_shape=tuple(shapes),
        grid=(M // tm,),
        in_specs=[row(W), tab, tab, pl.BlockSpec((1, HEAD_DIM), lambda i: (0, 0)),
                  pl.BlockSpec((3, HEAD_DIM), lambda i: (0, 0))],
        out_specs=tuple([row(NSA_HEADS * HEAD_DIM)] + [row(gw)] * 10),
        compiler_params=_cparams(("parallel",), 40),
        name="qkprep",
    )(qkv, cos, sin, q_norm_w.reshape(1, HEAD_DIM), k_norm_w)


def _rope_tables(pos):
    half = HEAD_DIM // 2
    inv = ROPE_THETA ** (-jnp.arange(half, dtype=F32) / half)
    ang = jnp.asarray(pos, F32)[:, None] * inv
    cos, sin = jnp.cos(ang), jnp.sin(ang)
    return jnp.concatenate([cos, cos], axis=-1), jnp.concatenate([-sin, sin], axis=-1)


def _compress_chunks(x, w_ref, n):
    xr = x.reshape(n, CMP_STRIDE, x.shape[-1])
    first = jnp.sum(xr * w_ref[0:CMP_STRIDE, :][None], axis=1)
    second = jnp.sum(xr * w_ref[CMP_STRIDE:CMP_LEN, :][None], axis=1)
    return first, second


def _finish_compress(first, second, b_ref, cos_ref, sin_ref, rope):
    n = first.shape[0]
    c = first + pltpu.roll(second, n - 1, 0) + b_ref[...]
    if not rope:
        return c
    parts = [_rope(c[:, g * HEAD_DIM:(g + 1) * HEAD_DIM], cos_ref[...], sin_ref[...]) for g in range(NSA_KV)]
    return jnp.concatenate(parts, axis=-1)


def _compress_body(x_ref, w_ref, b_ref, cos_ref, sin_ref, o_ref, *, rope):
    n = x_ref.shape[0] // CMP_STRIDE
    first, second = _compress_chunks(x_ref[...], w_ref, n)
    o_ref[...] = _finish_compress(first, second, b_ref, cos_ref, sin_ref, rope)


def _compress_prompt(x, w, b, cos_c, sin_c, rope):
    B, T, W = x.shape
    n = T // CMP_STRIDE
    full = lambda s: pl.BlockSpec(s, lambda i: (0,) * len(s))
    return pl.pallas_call(
        functools.partial(_compress_body, rope=rope),
        out_shape=jax.ShapeDtypeStruct((B, n, W), F32),
        grid=(B,),
        in_specs=[pl.BlockSpec((None, T, W), lambda i: (i, 0, 0)), full((CMP_LEN, W)), full((1, W)),
                  full((n, HEAD_DIM)), full((n, HEAD_DIM))],
        out_specs=pl.BlockSpec((None, n, W), lambda i: (i, 0, 0)),
        compiler_params=_cparams(("parallel",), 40),
        name="compress",
    )(x, w, b, cos_c, sin_c)


def _overlap_matrix(n_rows, n_cmp, n_cols, n_slc):
    cs = np.arange(n_rows)[:, None] * CMP_STRIDE
    ss = np.arange(n_cols)[None, :] * SLC_LEN
    m = (cs < ss + SLC_LEN) & (cs + CMP_LEN > ss)
    m &= (np.arange(n_rows)[:, None] < n_cmp) & (np.arange(n_cols)[None, :] < n_slc)
    return jnp.asarray(m.astype(np.float32), BF16)


def _nsa_prompt_body(q_ref, gate_ref, ck_ref, cv_ref, ks_ref, vs_ref, kw_ref, vw_ref, ov_ref, ex_ref, o_ref,
                     *, tq, T, n_cmp, n_slc, n_sel, span):
    t0 = pl.program_id(2) * tq
    scale = HEAD_DIM ** -0.5
    qpos = t0 + lax.broadcasted_iota(jnp.int32, (tq, 1), 0)
    qs = [q_ref[:, r * HEAD_DIM:(r + 1) * HEAD_DIM] for r in range(NSA_REP)]

    ncp = ck_ref.shape[0]
    nidx = lax.broadcasted_iota(jnp.int32, (1, ncp), 1)
    cmask = ((nidx * CMP_STRIDE + CMP_LEN - 1) <= qpos) & (nidx < n_cmp)
    ck = ck_ref[...]
    cv = cv_ref[...].astype(BF16)
    p_sum = jnp.zeros((tq, ncp), F32)
    o_cmp = []
    for r in range(NSA_REP):
        p = _masked_softmax_rows(_dot3(qs[r], ck, 1, 1) * scale, cmask)
        p_sum = p_sum + p
        o_cmp.append(_dot1(p, cv))

    imp = _dot_sel_rhs(p_sum, ov_ref[...])
    blk = lax.broadcasted_iota(jnp.int32, (1, n_slc), 1)
    cur = qpos // SLC_LEN
    visible = blk <= cur
    forced = (blk == 0) | (blk == cur) | (blk == cur - 1)
    score = jnp.where(forced, 1e9, jnp.where(visible, imp, -1e9))
    rank = _rank_desc(score, blk, n_slc)
    sel = jnp.where((rank < n_sel) & visible, 1.0, 0.0)
    kmask = _dot1(sel, ex_ref[...])

    kpos = lax.broadcasted_iota(jnp.int32, (1, T), 1)
    allow = (kmask > 0.5) & (kpos <= qpos)
    ks = ks_ref[...]
    vs = vs_ref[...]
    o_slc = []
    for r in range(NSA_REP):
        s = jnp.where(allow, _dot1(qs[r], ks, 1, 1) * scale, NEG)
        e = jnp.exp(s - jnp.max(s, axis=-1, keepdims=True))
        o_slc.append(_dot1(e, vs) / jnp.sum(e, axis=-1, keepdims=True))

    kstart = pl.multiple_of(jnp.clip(t0 - WINDOW, 0, T - span), tq)
    kw = kw_ref[pl.ds(kstart, span), :]
    vw = vw_ref[pl.ds(kstart, span), :]
    kposw = kstart + lax.broadcasted_iota(jnp.int32, (1, span), 1)
    allow_w = (kposw <= qpos) & (kposw >= qpos - WINDOW)
    gt = _sigmoid(gate_ref[...])
    for r in range(NSA_REP):
        s = jnp.where(allow_w, _dot1(qs[r], kw, 1, 1) * scale, NEG)
        e = jnp.exp(s - jnp.max(s, axis=-1, keepdims=True))
        o_win = _dot1(e, vw) / jnp.sum(e, axis=-1, keepdims=True)
        o = (gt[:, 3 * r:3 * r + 1] * o_cmp[r] + gt[:, 3 * r + 1:3 * r + 2] * o_slc[r]
             + gt[:, 3 * r + 2:3 * r + 3] * o_win)
        o_ref[:, r * HEAD_DIM:(r + 1) * HEAD_DIM] = o.astype(o_ref.dtype)


def _nsa_prompt(q, gate_g, ck, cv, ksb, vsb, kwb, vwb, B, T):
    tq = 128
    nq = T // tq
    n_cmp = (T - CMP_LEN) // CMP_STRIDE + 1
    ncp = T // CMP_STRIDE
    n_slc = -(-T // SLC_LEN)
    n_sel = min(SLC_TOPN, n_slc)
    span = min(WINDOW + tq, T)
    gw = NSA_REP * HEAD_DIM
    ov = _overlap_matrix(ncp, n_cmp, n_slc, n_slc)
    ex = jnp.asarray((np.arange(T)[None, :] // SLC_LEN == np.arange(n_slc)[:, None]).astype(np.float32), BF16)
    seq = lambda: pl.BlockSpec((None, T, HEAD_DIM), lambda b, g, i: (b, 0, g))
    cmp_spec = lambda: pl.BlockSpec((None, ncp, HEAD_DIM), lambda b, g, i: (b, 0, g))
    body = functools.partial(_nsa_prompt_body, tq=tq, T=T, n_cmp=n_cmp, n_slc=n_slc, n_sel=n_sel, span=span)
    return pl.pallas_call(
        body,
        out_shape=jax.ShapeDtypeStruct((B * T, NSA_HEADS * HEAD_DIM), BF16),
        grid=(B, NSA_KV, nq),
        in_specs=[pl.BlockSpec((tq, gw), lambda b, g, i: (b * nq + i, g)),
                  pl.BlockSpec((None, tq, 3 * NSA_REP), lambda b, g, i: (g, b * nq + i, 0)),
                  cmp_spec(), cmp_spec(), seq(), seq(), seq(), seq(),
                  pl.BlockSpec((ncp, n_slc), lambda b, g, i: (0, 0)),
                  pl.BlockSpec((n_slc, T), lambda b, g, i: (0, 0))],
        out_specs=pl.BlockSpec((tq, gw), lambda b, g, i: (b * nq + i, g)),
        compiler_params=_cparams(("parallel", "parallel", "arbitrary"), 48),
        name="nsa_prompt",
    )(q, gate_g, ck, cv, ksb.reshape(B, T, -1), vsb.reshape(B, T, -1), kwb.reshape(B, T, -1),
      vwb.reshape(B, T, -1), ov, ex)


def _nsa_dec1_body(pt_ref, q_ref, *refs, n_cmp, n_slc, qpos):
    PG = PAGE_GROUP
    kp, vp = refs[:PG], refs[PG:2 * PG]
    wk, bk, wv, bv, cos_ref, sin_ref, ov_ref = refs[2 * PG:2 * PG + 7]
    ocmp_o, idx_o = refs[2 * PG + 7:2 * PG + 9]
    fk, sk, fv, sv = refs[2 * PG + 9:]
    pg = pl.program_id(1)
    rows_pp = kp[0].shape[0] // CMP_STRIDE
    for p in range(PG):
        base = pl.multiple_of((pg * PG + p) * rows_pp, rows_pp)
        f, s = _compress_chunks(kp[p][...], wk, rows_pp)
        fk[pl.ds(base, rows_pp), :] = f
        sk[pl.ds(base, rows_pp), :] = s
        f, s = _compress_chunks(vp[p][...], wv, rows_pp)
        fv[pl.ds(base, rows_pp), :] = f
        sv[pl.ds(base, rows_pp), :] = s

    @pl.when(pg == pl.num_programs(1) - 1)
    def _():
        scale = HEAD_DIM ** -0.5
        ncp = fk.shape[0]
        ck = _finish_compress(fk[...], sk[...], bk, cos_ref, sin_ref, True)
        cv = _finish_compress(fv[...], sv[...], bv, cos_ref, sin_ref, False)
        q8 = q_ref[...]
        nidx = lax.broadcasted_iota(jnp.int32, (1, ncp), 1)
        cmask = ((nidx * CMP_STRIDE + CMP_LEN - 1) <= qpos) & (nidx < n_cmp)
        rowi = lax.broadcasted_iota(jnp.int32, (NSA_HEADS, 1), 0)
        ocmp = jnp.zeros((NSA_HEADS, HEAD_DIM), F32)
        p2 = jnp.zeros((NSA_HEADS, ncp), F32)
        for g in range(NSA_KV):
            sl = slice(g * HEAD_DIM, (g + 1) * HEAD_DIM)
            p = _masked_softmax_rows(_dot3(q8, ck[:, sl], 1, 1) * scale, cmask)
            in_g = (rowi >= g * NSA_REP) & (rowi < (g + 1) * NSA_REP)
            ocmp = jnp.where(in_g, _dot1(p, cv[:, sl]), ocmp)
            ps = jnp.sum(jnp.where(in_g, p, 0.0), axis=0, keepdims=True)
            p2 = jnp.where(rowi == g, ps, p2)
        ocmp_o[...] = ocmp

        imp = _dot_sel_rhs(p2, ov_ref[...])
        nsp = ov_ref.shape[1]
        blk = lax.broadcasted_iota(jnp.int32, (1, nsp), 1)
        cur = qpos // SLC_LEN
        forced = (blk == 0) | (blk == cur) | (blk == cur - 1)
        score = jnp.where(forced, 1e9, jnp.where(blk <= cur, imp, -1e9))
        score = jnp.where(blk < n_slc, score, -2e9)
        rank = _rank_desc(score, blk, n_slc)
        kcol = lax.broadcasted_iota(jnp.int32, (SLC_TOPN, 1), 0).astype(F32)
        blkf = blk.astype(F32)
        for g in range(NSA_KV):
            hit = rank[g:g + 1, :] == kcol
            ids = jnp.sum(jnp.where(hit, blkf, 0.0), axis=1, keepdims=True)
            idx_o[g] = jnp.broadcast_to(ids, (SLC_TOPN, HEAD_DIM)).astype(jnp.int32)


def _nsa_dec1(q8, page_table, cache_k, cache_v, wk, bk, wv, bv, cos_c, sin_c, past_len):
    DB, n_pages = page_table.shape
    PG = PAGE_GROUP
    assert n_pages % PG == 0
    page = cache_k.shape[1]
    W = cache_k.shape[2]
    ncp = past_len // CMP_STRIDE
    n_cmp = (past_len + 1 - CMP_LEN) // CMP_STRIDE + 1
    n_slc = -(-(past_len + 1) // SLC_LEN)
    nsp = -(-n_slc // 128) * 128
    assert past_len // SLC_LEN + 1 == n_slc and n_slc >= SLC_TOPN
    ov = _overlap_matrix(ncp, n_cmp, nsp, n_slc)

    def page_spec(p):
        return pl.BlockSpec((None, page, W), lambda b, pg, pt: (pt[b, pg * PG + p], 0, 0))

    full = lambda s: pl.BlockSpec(s, lambda b, pg, pt: (0,) * len(s))
    in_specs = [pl.BlockSpec((None, NSA_HEADS, HEAD_DIM), lambda b, pg, pt: (b, 0, 0))]
    in_specs += [page_spec(p) for p in range(PG)] * 2
    in_specs += [full((CMP_LEN, W)), full((1, W)), full((CMP_LEN, W)), full((1, W)),
                 full((ncp, HEAD_DIM)), full((ncp, HEAD_DIM)), full((ncp, nsp))]
    grid_spec = pltpu.PrefetchScalarGridSpec(
        num_scalar_prefetch=1, grid=(DB, n_pages // PG), in_specs=in_specs,
        out_specs=(pl.BlockSpec((None, NSA_HEADS, HEAD_DIM), lambda b, pg, pt: (b, 0, 0)),
                   pl.BlockSpec((None, NSA_KV, SLC_TOPN, HEAD_DIM), lambda b, pg, pt: (b, 0, 0, 0))),
        scratch_shapes=[pltpu.VMEM((ncp, W), F32)] * 4)
    body = functools.partial(_nsa_dec1_body, n_cmp=n_cmp, n_slc=n_slc, qpos=past_len)
    return pl.pallas_call(
        body,
        out_shape=(jax.ShapeDtypeStruct((DB, NSA_HEADS, HEAD_DIM), F32),
                   jax.ShapeDtypeStruct((DB, NSA_KV, SLC_TOPN, HEAD_DIM), jnp.int32)),
        grid_spec=grid_spec,
        compiler_params=_cparams(("parallel", "arbitrary"), 40),
        name="nsa_dec_cmp",
    )(page_table, q8, *([cache_k] * PG), *([cache_v] * PG), wk, bk, wv, bv, cos_c, sin_c, ov)


def _attend_rows(q, qb, keys, k_new, vals, v_new, valid, scale):
    s_new = jnp.sum(q * k_new, axis=-1, keepdims=True) * scale
    s_blk = []
    m = s_new
    for k_ref, ok in zip(keys, valid):
        s = _dg(qb, k_ref[...].astype(BF16), 1, 1) * scale
        if ok is not None:
            s = jnp.where(ok, s, NEG)
        s_blk.append(s)
        m = jnp.maximum(m, jnp.max(s, axis=-1, keepdims=True))
    e_new = jnp.exp(s_new - m)
    den = e_new
    acc = e_new * v_new
    for s, v_ref in zip(s_blk, vals):
        e = jnp.exp(s - m)
        den = den + jnp.sum(e, axis=-1, keepdims=True)
        acc = acc + _dot1(e, v_ref[...])
    return acc / den


def _shift_in(buf, new_row):
    n = buf.shape[0]
    rolled = pltpu.roll(buf, n - 1, 0)
    rowi = lax.broadcasted_iota(jnp.int32, (n, 1), 0)
    return jnp.where(rowi == n - 1, new_row, rolled)


def _nsa_dec2_body(pt_ref, idx_ref, q_ref, gate_ref, ocmp_ref, *refs, n_slc):
    n = SLC_TOPN
    kb, vb = refs[:n], refs[n:2 * n]
    kn, vn, wk, wv, wkn, wvn = refs[2 * n:2 * n + 6]
    o_ref, wko, wvo = refs[2 * n + 6:]
    b = pl.program_id(0)
    g = pl.program_id(1)
    scale = HEAD_DIM ** -0.5
    q = q_ref[...]
    qb = q.astype(BF16)
    valid = [idx_ref[(b * NSA_KV + g) * n + k] < n_slc - 1 for k in range(n)]
    o_slc = _attend_rows(q, qb, kb, kn[...], vb, vn[...], valid, scale)
    o_win = _attend_rows(q, qb, [wk], wkn[...], [wv], wvn[...], [None], scale)
    gt = _sigmoid(gate_ref[...])
    o_ref[...] = gt[:, 0:1] * ocmp_ref[...] + gt[:, 1:2] * o_slc + gt[:, 2:3] * o_win
    wko[...] = _shift_in(wk[...], wkn[...])
    wvo[...] = _shift_in(wv[...], wvn[...])


def _nsa_dec2(q4, gate4, ocmp4, page_table, idx, slc_k, slc_v, k_new, v_new, win_k, win_v, wk_new, wv_new,
              past_len):
    DB = q4.shape[0]
    n = SLC_TOPN
    n_slc = -(-(past_len + 1) // SLC_LEN)
    half = slc_k.shape[1] // SLC_LEN
    W = slc_k.shape[2]
    hk = slc_k.reshape(slc_k.shape[0] * half, SLC_LEN, W)
    hv = slc_v.reshape(slc_v.shape[0] * half, SLC_LEN, W)
    win = win_k.shape[1]

    def blk_spec(k):
        def imap(b, g, pt, ix):
            s = jnp.minimum(ix[(b * NSA_KV + g) * n + k], n_slc - 2)
            return (pt[b, s // half] * half + s % half, 0, g)
        return pl.BlockSpec((None, SLC_LEN, HEAD_DIM), imap)

    head = lambda w: pl.BlockSpec((None, None, NSA_REP, w), lambda b, g, pt, ix: (b, g, 0, 0))
    new = lambda: pl.BlockSpec((None, 1, HEAD_DIM), lambda b, g, pt, ix: (b, 0, g))
    wspec = lambda: pl.BlockSpec((None, win, HEAD_DIM), lambda b, g, pt, ix: (b, 0, g))
    in_specs = [head(HEAD_DIM), head(3), head(HEAD_DIM)] + [blk_spec(k) for k in range(n)] * 2 \
        + [new(), new(), wspec(), wspec(), new(), new()]
    grid_spec = pltpu.PrefetchScalarGridSpec(
        num_scalar_prefetch=2, grid=(DB, NSA_KV), in_specs=in_specs,
        out_specs=(head(HEAD_DIM), wspec(), wspec()))
    return pl.pallas_call(
        functools.partial(_nsa_dec2_body, n_slc=n_slc),
        out_shape=(jax.ShapeDtypeStruct((DB, NSA_KV, NSA_REP, HEAD_DIM), F32),
                   jax.ShapeDtypeStruct(win_k.shape, F32), jax.ShapeDtypeStruct(win_v.shape, F32)),
        grid_spec=grid_spec,
        compiler_params=_cparams(("parallel", "arbitrary"), 40),
        name="nsa_dec_attend",
    )(page_table, idx.reshape(-1), q4, gate4, ocmp4, *([hk] * n), *([hv] * n),
      k_new, v_new, win_k, win_v, wk_new, wv_new)


def _dn_gates(sm, al_ref, dtb_ref):
    g = -jnp.exp(al_ref[...]) * _softplus(sm + dtb_ref[...])
    return g, _sigmoid(sm)


def _dn_out(o, z, nw_ref):
    return _rms(o, nw_ref[...]) * _silu(z)


def _dn_prompt_body(x_ref, z_ref, sm_ref, aT_ref, cw_ref, al_ref, alT_ref, dtb_ref, dtbT_ref, nw_ref,
                    o_ref, s_out_ref, xbuf, S):
    C = DN_CHUNK
    c = pl.program_id(1)

    @pl.when(c == 0)
    def _():
        xbuf[0:8, :] = jnp.zeros((8, xbuf.shape[1]), F32)
        S[...] = jnp.zeros(S.shape, F32)

    xbuf[8:8 + C, :] = x_ref[...]
    lo = 8 - (DN_CONV - 1)
    y = cw_ref[0:1, :] * xbuf[lo:lo + C, :]
    for j in range(1, DN_CONV):
        y = y + cw_ref[j:j + 1, :] * xbuf[lo + j:lo + j + C, :]
    xbuf[0:8, :] = xbuf[C:C + 8, :]
    y = _silu(y)

    g, beta = _dn_gates(sm_ref[...], al_ref, dtb_ref)
    gT = -jnp.exp(alT_ref[...]) * _softplus(aT_ref[...] + dtbT_ref[...])
    row = lax.broadcasted_iota(jnp.int32, (C, C), 0)
    col = lax.broadcasted_iota(jnp.int32, (C, C), 1)
    tril = row >= col
    strict = row > col
    G = _dot_sel_lhs(jnp.where(tril, 1.0, 0.0).astype(BF16), g)
    GT = _dot_sel_rhs(gT, jnp.where(row <= col, 1.0, 0.0).astype(BF16))
    eye = jnp.where(row == col, 1.0, 0.0)
    pair = jnp.right_shift(row, 1) == jnp.right_shift(col, 1)
    merges = []
    for lvl in range(1, int(np.log2(C))):
        merges.append((jnp.right_shift(row, lvl + 1) == jnp.right_shift(col, lvl + 1))
                      & (jnp.right_shift(row, lvl) != jnp.right_shift(col, lvl)))
    nh = DN_HEADS
    for h in range(nh):
        q = _l2(y[:, h * DN_DK:(h + 1) * DN_DK])
        k = _l2(y[:, (nh + h) * DN_DK:(nh + h + 1) * DN_DK])
        v = y[:, 2 * nh * DN_DK + h * DN_DV:2 * nh * DN_DK + (h + 1) * DN_DV]
        Gc = G[:, DNA_COL + h:DNA_COL + h + 1]
        Gr = GT[h:h + 1, :]
        bc = beta[:, DNB_COL + h:DNB_COL + h + 1]
        decay = jnp.where(tril, jnp.exp(jnp.where(tril, Gc - Gr, 0.0)), 0.0)
        qc = q * (DN_DK ** -0.5)
        kb = k * bc
        M = jnp.where(strict, _dot3(kb, k, 1, 1) * decay, 0.0)
        P = eye - jnp.where(pair, M, 0.0)
        for off in merges:
            P = P - _dot3(P, _dot3(jnp.where(off, M, 0.0), P))
        eG = jnp.exp(Gc)
        u = _dot3(P, v * bc)
        w = _dot3(P, kb * eG)
        aqk = jnp.where(tril, _dot3(qc, k, 1, 1) * decay, 0.0)
        Sh = S[h]
        v_new = u - _dot3(w, Sh)
        o = _dot3(qc * eG, Sh) + _dot3(aqk, v_new)
        Gl = Gc[C - 1:C, :]
        kdec = k * jnp.exp(Gl - Gc)
        S[h] = Sh * jnp.exp(Gl) + _dot3(kdec.T, v_new)
        o_ref[:, h * DN_DV:(h + 1) * DN_DV] = _dn_out(o, z_ref[:, h * DN_DV:(h + 1) * DN_DV], nw_ref).astype(o_ref.dtype)

    @pl.when(c == pl.num_programs(1) - 1)
    def _():
        s_out_ref[...] = S[...]


def _dn_gate_rows(a_log, dt_bias):
    pad = lambda v: jnp.zeros((1, 128), F32).at[0, DNA_COL:DNA_COL + DN_HEADS].set(v)
    return pad(a_log), pad(dt_bias)


def _dn_prompt(dnx, z, small, conv_w, a_log, dt_bias, norm_w, B, T):
    C = DN_CHUNK
    assert T % C == 0
    nc = T // C
    Wx = dnx.shape[1]
    Wz = z.shape[1]
    aT = small[:, DNA_COL:DNA_COL + DN_HEADS].reshape(B, T, DN_HEADS).transpose(0, 2, 1)
    al, dtb = _dn_gate_rows(a_log, dt_bias)
    full = lambda s: pl.BlockSpec(s, lambda b, c: (0,) * len(s))
    row = lambda w: pl.BlockSpec((C, w), lambda b, c: (b * nc + c, 0))
    return pl.pallas_call(
        _dn_prompt_body,
        out_shape=(jax.ShapeDtypeStruct((B * T, Wz), BF16),
                   jax.ShapeDtypeStruct((B, DN_HEADS, DN_DK, DN_DV), F32)),
        grid=(B, nc),
        in_specs=[row(Wx), row(Wz), row(128), pl.BlockSpec((None, DN_HEADS, C), lambda b, c: (b, 0, c)),
                  full((DN_CONV, Wx)), full((1, 128)), full((DN_HEADS, 1)), full((1, 128)), full((DN_HEADS, 1)),
                  full((1, DN_DV))],
        out_specs=(row(Wz), pl.BlockSpec((None, DN_HEADS, DN_DK, DN_DV), lambda b, c: (b, 0, 0, 0))),
        scratch_shapes=[pltpu.VMEM((C + 8, Wx), F32), pltpu.VMEM((DN_HEADS, DN_DK, DN_DV), F32)],
        compiler_params=_cparams(("parallel", "arbitrary"), 48),
        name="deltanet_prompt",
    )(dnx, z, small, aT, conv_w, al, a_log.reshape(DN_HEADS, 1), dtb, dt_bias.reshape(DN_HEADS, 1),
      norm_w.reshape(1, DN_DV))


def _dn_dec_body(x_ref, hist_ref, sm_ref, z_ref, s0_ref, cw_ref, al_ref, dtb_ref, nw_ref,
                 o_ref, conv_o, s_o):
    x = x_ref[...]
    hist = hist_ref[...]
    y = cw_ref[DN_CONV - 1:DN_CONV, :] * x
    for j in range(DN_CONV - 1):
        y = y + cw_ref[j:j + 1, :] * hist[j:j + 1, :]
    conv_o[0:DN_CONV - 2, :] = hist[1:DN_CONV - 1, :]
    conv_o[DN_CONV - 2:DN_CONV - 1, :] = x
    y = _silu(y)
    g, beta = _dn_gates(sm_ref[...], al_ref, dtb_ref)
    n = DN_DK
    diag = lax.broadcasted_iota(jnp.int32, (n, n), 0) == lax.broadcasted_iota(jnp.int32, (n, n), 1)

    def column(r):
        return jnp.sum(jnp.where(diag, jnp.broadcast_to(r, (n, n)), 0.0), axis=1, keepdims=True)

    nh = DN_HEADS
    scale = DN_DK ** -0.5
    for h in range(nh):
        q = _l2(y[:, h * DN_DK:(h + 1) * DN_DK])
        k = _l2(y[:, (nh + h) * DN_DK:(nh + h + 1) * DN_DK])
        v = y[:, 2 * nh * DN_DK + h * DN_DV:2 * nh * DN_DK + (h + 1) * DN_DV]
        eg = jnp.exp(g[:, DNA_COL + h:DNA_COL + h + 1])
        bt = beta[:, DNB_COL + h:DNB_COL + h + 1]
        kcol = column(k)
        Sh = s0_ref[h]
        kS = jnp.sum(Sh * kcol, axis=0, keepdims=True)
        qS = jnp.sum(Sh * column(q), axis=0, keepdims=True)
        v_new = bt * (v - eg * kS)
        o = scale * (eg * qS + jnp.sum(q * k, axis=-1, keepdims=True) * v_new)
        s_o[h] = Sh * eg + kcol * v_new
        o_ref[:, h * DN_DV:(h + 1) * DN_DV] = _dn_out(o, z_ref[:, h * DN_DV:(h + 1) * DN_DV], nw_ref).astype(o_ref.dtype)


def _dn_decode(dnx, hist, small, z, s0, conv_w, a_log, dt_bias, norm_w):
    DB, Wx = dnx.shape
    Wz = z.shape[1]
    al, dtb = _dn_gate_rows(a_log, dt_bias)
    full = lambda s: pl.BlockSpec(s, lambda b: (0,) * len(s))
    one = lambda w: pl.BlockSpec((None, 1, w), lambda b: (b, 0, 0))
    nh = DN_CONV - 1
    st = pl.BlockSpec((None, DN_HEADS, DN_DK, DN_DV), lambda b: (b, 0, 0, 0))
    return pl.pallas_call(
        _dn_dec_body,
        out_shape=(jax.ShapeDtypeStruct((DB, 1, Wz), BF16), jax.ShapeDtypeStruct((DB, nh, Wx), F32),
                   jax.ShapeDtypeStruct(s0.shape, F32)),
        grid=(DB,),
        in_specs=[one(Wx), pl.BlockSpec((None, nh, Wx), lambda b: (b, 0, 0)), one(128), one(Wz), st,
                  full((DN_CONV, Wx)), full((1, 128)), full((1, 128)), full((1, DN_DV))],
        out_specs=(one(Wz), pl.BlockSpec((None, nh, Wx), lambda b: (b, 0, 0)), st),
        compiler_params=_cparams(("parallel",), 40),
        name="deltanet_decode",
    )(dnx.reshape(DB, 1, Wx), hist, small.reshape(DB, 1, 128), z.reshape(DB, 1, Wz), s0, conv_w, al, dtb,
      norm_w.reshape(1, DN_DV))


def _top_ranks(x, rows, n_rows):
    rank = jnp.full(x.shape, float(PEER_TOPK), F32)
    vals = []
    for k in range(PEER_TOPK):
        m = jnp.max(x, axis=0, keepdims=True)
        first = jnp.min(jnp.where(x == m, rows, n_rows), axis=0, keepdims=True)
        hit = rows == first
        rank = jnp.where(hit, float(k), rank)
        x = jnp.where(hit, -jnp.inf, x)
        vals.append(m)
    return jnp.concatenate(vals, axis=0), rank


def _peer_route_body(qh_ref, keys_ref, r2_o, e2_o, c1_o, e1z_o):
    tr = qh_ref.shape[0]
    K = PEER_TOPK
    half = PEER_DKEY // 2
    rows = lax.broadcasted_iota(jnp.int32, (PEER_NKEYS, tr), 0)
    crow = lax.broadcasted_iota(jnp.int32, (K * K, tr), 0)
    for h in range(PEER_HEADS):
        s1 = _dot3(keys_ref[h, 0], qh_ref[:, h * PEER_DKEY:h * PEER_DKEY + half], 1, 1)
        s2 = _dot3(keys_ref[h, 1], qh_ref[:, h * PEER_DKEY + half:(h + 1) * PEER_DKEY], 1, 1)
        v1, r1 = _top_ranks(s1, rows, PEER_NKEYS)
        v2, r2 = _top_ranks(s2, rows, PEER_NKEYS)
        cand = jnp.concatenate([v1[a:a + 1, :] + v2 for a in range(K)], axis=0)
        e1v = jnp.exp(v1 - v1[0:1, :])
        e2v = jnp.exp(v2 - v2[0:1, :])
        wmat = jnp.concatenate([e1v[a:a + 1, :] * e2v for a in range(K)], axis=0)
        chosen = jnp.zeros(cand.shape, F32)
        for _ in range(K):
            m = jnp.max(cand, axis=0, keepdims=True)
            first = jnp.min(jnp.where(cand == m, crow, K * K), axis=0, keepdims=True)
            hit = crow == first
            chosen = jnp.where(hit, 1.0, chosen)
            cand = jnp.where(hit, -jnp.inf, cand)
        z = jnp.sum(chosen * wmat, axis=0, keepdims=True)
        c1 = jnp.zeros(s1.shape, F32)
        for a in range(K):
            cnt = jnp.sum(chosen[a * K:(a + 1) * K, :], axis=0, keepdims=True)
            c1 = jnp.where(r1 == float(a), cnt, c1)
        r2_o[h] = r2.astype(r2_o.dtype)
        e2_o[h] = jnp.exp(s2 - v2[0:1, :]).astype(e2_o.dtype)
        c1_o[h] = c1
        e1z_o[h] = jnp.exp(s1 - v1[0:1, :]) / z


def _peer_route(qh, keys):
    N = qh.shape[0]
    tr = min(N, 256)
    spec = pl.BlockSpec((PEER_HEADS, PEER_NKEYS, tr), lambda i: (0, 0, i))
    shp = lambda dt: jax.ShapeDtypeStruct((PEER_HEADS, PEER_NKEYS, N), dt)
    return pl.pallas_call(
        _peer_route_body,
        out_shape=(shp(BF16), shp(BF16), shp(F32), shp(F32)),
        grid=(N // tr,),
        in_specs=[pl.BlockSpec((tr, qh.shape[1]), lambda i: (i, 0)),
                  pl.BlockSpec(keys.shape, lambda i: (0, 0, 0, 0))],
        out_specs=(spec, spec, spec, spec),
        compiler_params=_cparams(("parallel",), 40),
        name="peer_route",
    )(qh, keys)


def _peer_dense_body(h_ref, u_ref, v_ref, r2_ref, e2_ref, c1_ref, e1z_ref, o_ref, acc):
    j = pl.program_id(1)
    te = u_ref.shape[0]
    groups = te // PEER_NKEYS

    @pl.when(j == 0)
    def _():
        acc[...] = jnp.zeros(acc.shape, F32)

    act = _gelu_tanh(_dg(u_ref[...], h_ref[...], 1, 1))
    parts = []
    for ii in range(groups):
        irow = j * groups + ii
        w = jnp.zeros((PEER_NKEYS, h_ref.shape[0]), BF16)
        for h in range(PEER_HEADS):
            c1 = c1_ref[h, pl.ds(irow, 1), :].astype(BF16)
            e1 = e1z_ref[h, pl.ds(irow, 1), :].astype(BF16)
            w = w + jnp.where(r2_ref[h] < c1, e2_ref[h] * e1, jnp.zeros_like(w))
        parts.append(w.astype(F32) * act[ii * PEER_NKEYS:(ii + 1) * PEER_NKEYS, :])
    x = jnp.concatenate(parts, axis=0)
    acc[...] += _dg(x.T.astype(BF16), v_ref[...])

    @pl.when(j == pl.num_programs(1) - 1)
    def _():
        o_ref[...] = acc[...]


def _peer_dense(h2, u, v, r2, e2, c1, e1z, tm):
    N, D = h2.shape
    NE = u.shape[0]
    te = 512
    tm = min(tm, N)
    tok = lambda: pl.BlockSpec((PEER_HEADS, PEER_NKEYS, tm), lambda i, j: (0, 0, i))
    return pl.pallas_call(
        _peer_dense_body,
        out_shape=jax.ShapeDtypeStruct((N, D), F32),
        grid=(N // tm, NE // te),
        in_specs=[pl.BlockSpec((tm, D), lambda i, j: (i, 0)), pl.BlockSpec((te, D), lambda i, j: (j, 0)),
                  pl.BlockSpec((te, D), lambda i, j: (j, 0)), tok(), tok(), tok(), tok()],
        out_specs=pl.BlockSpec((tm, D), lambda i, j: (i, 0)),
        scratch_shapes=[pltpu.VMEM((tm, D), F32)],
        compiler_params=_cparams(("parallel", "arbitrary"), 56),
        name="peer_dense",
    )(h2, u, v, r2, e2, c1, e1z)


def _peer(h2, wq, keys, u, v, tm):
    qh = _matmul(h2, wq, tn=1024, name="peer_query")
    r2, e2, c1, e1z = _peer_route(qh, keys)
    return _peer_dense(h2, u, v, r2, e2, c1, e1z, tm)


def _in_proj_weights(w_in):
    D = w_in.shape[0]
    nq = NSA_HEADS * HEAD_DIM
    nkv = 6 * NSA_KV * HEAD_DIM
    ng = NSA_HEADS * 3
    ndn = DN_HEADS * (2 * DN_DK + DN_DV)
    nz = DN_HEADS * DN_DV
    o_gate = nq + nkv
    o_dn = o_gate + ng
    o_a = o_dn + ndn
    o_b = o_a + DN_HEADS
    o_z = o_b + DN_HEADS
    o_m = o_z + nz
    small = jnp.concatenate([w_in[:, o_gate:o_dn], w_in[:, o_a:o_z],
                             jnp.zeros((D, 128 - ng - 2 * DN_HEADS), w_in.dtype)], axis=1)
    cast = lambda a: a.astype(BF16)
    return (cast(w_in[:, :o_gate]), cast(small), cast(w_in[:, o_dn:o_a]), cast(w_in[:, o_z:o_m]),
            cast(w_in[:, o_m:]))


def _project(h, wts):
    w_qkv, w_small, w_dn, w_z, w_merge = wts
    return (_matmul(h, w_qkv, tn=512, name="proj_qkv"), _matmul(h, w_small, tn=128, name="proj_small"),
            _matmul(h, w_dn, tn=1024, name="proj_dn"), _matmul(h, w_z, tn=1024, name="proj_z"),
            _matmul(h, w_merge, tn=1024, name="proj_merge"))


def _gate_groups(small, M):
    return small[:, GATE_COL:GATE_COL + 3 * NSA_HEADS].reshape(M, NSA_KV, 3 * NSA_REP).transpose(1, 0, 2)


def kernel(x_prompt, x_sample, cache_cmp_k, cache_cmp_v, cache_slc_k, cache_slc_v, state_win_k, state_win_v,
           state_conv, state_delta, page_table, c_prompt, c_sample, w_ada, b_ada, norm1_w, norm2_w, w_in,
           q_norm_w, k_norm_w, w_cmp_k, b_cmp_k, w_cmp_v, b_cmp_v, dn_conv_w, dn_A_log, dn_dt_bias, dn_norm_w,
           w_br_a, w_br_b, w_out, w_peer_q, w_peer_keys, w_peer_u, w_peer_v):
    depth = w_in.shape[0]
    B, T, D = x_prompt.shape
    DB = x_sample.shape[0]
    assert x_sample.shape[1] == 1
    NP = B * T
    page = cache_cmp_k.shape[2]
    past_len = page_table.shape[1] * page
    gw = NSA_KV * HEAD_DIM

    yp = x_prompt.reshape(NP, D)
    ys = x_sample.reshape(DB, D)
    p_hist, s_hist = [], []
    cos_p, sin_p = _rope_tables(np.arange(T))
    cos_s, sin_s = _rope_tables(np.full((DB,), past_len))
    n_cp = T // CMP_STRIDE
    cos_cp, sin_cp = _rope_tables(np.arange(n_cp) * CMP_STRIDE + CMP_LEN - 1)
    n_cs = past_len // CMP_STRIDE
    cos_cs, sin_cs = _rope_tables(np.arange(n_cs) * CMP_STRIDE + CMP_LEN - 1)
    n_ada = -(-(B + DB) // 16) * 16
    tile2 = lambda a: jnp.concatenate([a] * NSA_KV, axis=-1)

    for l in range(depth):
        c_all = jnp.concatenate([c_prompt, c_sample, jnp.zeros((n_ada - B - DB, D), F32)], axis=0)
        ada = _matmul(c_all, w_ada[l], tn=512, bias=b_ada[l], name="adaln")
        mods = [ada[:, i * D:(i + 1) * D] for i in range(6)]
        mp = [m[:B] for m in mods]
        ms = [m[B:B + DB] for m in mods]

        wts = _in_proj_weights(w_in[l])
        wa, wb, wo = w_br_a[l].astype(BF16), w_br_b[l].astype(BF16), w_out[l].astype(BF16)
        wq = w_peer_q[l].transpose(1, 0, 2).reshape(D, PEER_HEADS * PEER_DKEY).astype(BF16)
        pu, pv = w_peer_u[l].astype(BF16), w_peer_v[l].astype(BF16)
        wck, wcv = tile2(w_cmp_k[l]), tile2(w_cmp_v[l])
        bck, bcv = tile2(b_cmp_k[l]).reshape(1, gw), tile2(b_cmp_v[l]).reshape(1, gw)

        hp = _normmod(yp, norm1_w[l], mp[1], mp[0], T)
        qkv, small, dnx, z, mg = _project(hp, wts)
        q, kc, vc, ks, vs, kw, vw, ksb, vsb, kwb, vwb = _qkprep(qkv, cos_p, sin_p, q_norm_w[l], k_norm_w[l], T)
        ck = _compress_prompt(kc.reshape(B, T, gw), wck, bck, cos_cp, sin_cp, True)
        cv = _compress_prompt(vc.reshape(B, T, gw), wcv, bcv, cos_cp, sin_cp, False)
        o_nsa = _nsa_prompt(q, _gate_groups(small, NP), ck, cv, ksb, vsb, kwb, vwb, B, T)
        o_dn, p_delta = _dn_prompt(dnx, z, small, dn_conv_w[l], dn_A_log[l], dn_dt_bias[l], dn_norm_w[l], B, T)
        mixed = _merge(o_nsa, o_dn, mg, wa, wb)
        x1p, h2p = _outproj(yp, mixed, wo, mp[2], norm2_w[l], mp[4], mp[3], T)

        keep = min(WINDOW, T)
        r5 = lambda a: a.reshape(B, T, NSA_KV, HEAD_DIM)
        p_conv = dnx.reshape(B, T, -1)[:, T - (DN_CONV - 1):]
        p_hist.append((r5(kc), r5(vc), r5(ks), r5(vs), r5(kw)[:, T - keep:], r5(vw)[:, T - keep:], p_conv, p_delta))

        hs = _normmod(ys, norm1_w[l], ms[1], ms[0], 1)
        qkv, small, dnx, z, mg = _project(hs, wts)
        q, kc, vc, ks, vs, kw, vw, _, _, _, _ = _qkprep(qkv, cos_s, sin_s, q_norm_w[l], k_norm_w[l], 1)
        pool = cache_cmp_k.shape[1]
        flat = lambda c: c[l].reshape(pool, page, gw)
        ocmp, idx = _nsa_dec1(q.reshape(DB, NSA_HEADS, HEAD_DIM), page_table, flat(cache_cmp_k), flat(cache_cmp_v),
                              wck, bck, wcv, bcv, cos_cs, sin_cs, past_len)
        win = state_win_k.shape[2]
        gate4 = small[:, GATE_COL:GATE_COL + 3 * NSA_HEADS].reshape(DB, NSA_KV, NSA_REP, 3)
        r4 = lambda a: a.reshape(DB, NSA_KV, NSA_REP, HEAD_DIM)
        r3 = lambda a: a.reshape(DB, 1, gw)
        o4, s_win_k, s_win_v = _nsa_dec2(
            r4(q), gate4, r4(ocmp), page_table, idx[..., 0], flat(cache_slc_k), flat(cache_slc_v), r3(ks), r3(vs),
            state_win_k[l].reshape(DB, win, gw), state_win_v[l].reshape(DB, win, gw), r3(kw), r3(vw), past_len)
        o_dn, s_conv, s_delta = _dn_decode(dnx, state_conv[l], small, z, state_delta[l], dn_conv_w[l], dn_A_log[l],
                                           dn_dt_bias[l], dn_norm_w[l])
        mixed = _merge(o4.reshape(DB, NSA_HEADS * HEAD_DIM), o_dn.reshape(DB, -1), mg, wa, wb)
        x1s, h2s = _outproj(ys, mixed, wo, ms[2], norm2_w[l], ms[4], ms[3], 1)
        r5s = lambda a: a.reshape(DB, 1, NSA_KV, HEAD_DIM)
        s_hist.append((r5s(kc), r5s(vc), r5s(ks), r5s(vs), s_win_k.reshape(DB, win, NSA_KV, HEAD_DIM),
                       s_win_v.reshape(DB, win, NSA_KV, HEAD_DIM), s_conv, s_delta))

        peer_p = _peer(h2p, wq, w_peer_keys[l], pu, pv, 512)
        n_pad = -(-DB // 128) * 128
        h2s_pad = jnp.concatenate([h2s, jnp.zeros((n_pad - DB, D), h2s.dtype)], axis=0)
        peer_s = _peer(h2s_pad, wq, w_peer_keys[l], pu, pv, n_pad)[:DB]
        yp = _residual(x1p, peer_p, mp[5], T)
        ys = _residual(x1s, peer_s, ms[5], 1)

    stack = lambda hist: [jnp.stack(a) for a in zip(*hist)]
    return (yp.reshape(B, T, D), ys.reshape(DB, 1, D), *stack(p_hist), *stack(s_hist))
```

```python
import functools

import numpy as np
import jax
import jax.numpy as jnp
from jax import lax
from jax.experimental import pallas as pl
from jax.experimental.pallas import tpu as pltpu

F32 = jnp.float32
BF16 = jnp.bfloat16

HEAD_DIM = 128
NSA_HEADS = 8
NSA_KV = 2
NSA_REP = NSA_HEADS // NSA_KV
CMP_STRIDE = 16
CMP_LEN = 2 * CMP_STRIDE
SLC_LEN = 64
SLC_TOPN = 16
WINDOW = 512
DN_HEADS = 8
DN_DK = 128
DN_DV = 128
DN_CONV = 4
DN_CHUNK = 128
PEER_HEADS = 8
PEER_NKEYS = 128
PEER_DKEY = 256
PEER_TOPK = 16
ROPE_THETA = 10000.0
EPS = 1e-6
NEG = -1e30
PAGE_GROUP = 8
GATE_COL = 0
DNA_COL = 24
DNB_COL = 32
MIB = 2 ** 20


def _cparams(semantics, vmem_mib):
    return pltpu.CompilerParams(dimension_semantics=semantics, vmem_limit_bytes=vmem_mib * MIB)


def _dg(a, b, ca=1, cb=0):
    return lax.dot_general(a, b, (((ca,), (cb,)), ((), ())), preferred_element_type=F32)


def _dot1(a, b, ca=1, cb=0):
    return _dg(a.astype(BF16), b.astype(BF16), ca, cb)


def _split2(a):
    hi = a.astype(BF16)
    return hi, (a - hi.astype(F32)).astype(BF16)


def _split3(a):
    hi = a.astype(BF16)
    r = a - hi.astype(F32)
    mid = r.astype(BF16)
    return hi, mid, (r - mid.astype(F32)).astype(BF16)


def _dot3(a, b, ca=1, cb=0):
    ah, al = _split2(a)
    bh, bl = _split2(b)
    return _dg(ah, bh, ca, cb) + _dg(ah, bl, ca, cb) + _dg(al, bh, ca, cb)


def _dot_sel_rhs(a, sel):
    return sum(_dg(p, sel) for p in _split3(a))


def _dot_sel_lhs(sel, b):
    return sum(_dg(sel, p) for p in _split3(b))


def _sigmoid(x):
    return 1.0 / (1.0 + jnp.exp(-x))


def _silu(x):
    return x * _sigmoid(x)


def _softplus(x):
    return jnp.maximum(x, 0.0) + jnp.log(1.0 + jnp.exp(-jnp.abs(x)))


def _gelu_tanh(x):
    return 0.5 * x * (1.0 + jnp.tanh(0.7978845608028654 * (x + 0.044715 * (x * x * x))))


def _rms(x, w):
    return x * lax.rsqrt(jnp.mean(x * x, axis=-1, keepdims=True) + EPS) * w


def _l2(x):
    return x * lax.rsqrt(jnp.sum(x * x, axis=-1, keepdims=True) + EPS)


def _rope(x, cos, sin_signed):
    return x * cos + pltpu.roll(x, HEAD_DIM // 2, 1) * sin_signed


def _masked_softmax_rows(s, mask):
    s = jnp.where(mask, s, NEG)
    m = jnp.max(s, axis=-1, keepdims=True)
    e = jnp.where(mask, jnp.exp(s - m), 0.0)
    den = jnp.sum(e, axis=-1, keepdims=True)
    return e / jnp.where(den > 0.0, den, 1.0)


def _rank_desc(score, lane_idx, n):
    rank = jnp.zeros(score.shape, F32)
    for s2 in range(n):
        col = score[:, s2:s2 + 1]
        beats = (col > score) | ((col == score) & (lane_idx > s2))
        rank = rank + jnp.where(beats, 1.0, 0.0)
    return rank


def _mm_body(*refs, has_bias):
    if has_bias:
        a_ref, b_ref, bias_ref, o_ref = refs
    else:
        a_ref, b_ref, o_ref = refs
    acc = _dot1(a_ref[...], b_ref[...])
    if has_bias:
        acc = acc + bias_ref[...]
    o_ref[...] = acc.astype(o_ref.dtype)


def _matmul(a, b, *, tn, bias=None, out_dtype=F32, tm_max=1024, name="matmul"):
    M, K = a.shape
    N = b.shape[1]
    tm = min(M, tm_max)
    tn = min(tn, N)
    assert M % tm == 0 and N % tn == 0
    in_specs = [pl.BlockSpec((tm, K), lambda i, j: (i, 0)), pl.BlockSpec((K, tn), lambda i, j: (0, j))]
    args = [a, b]
    if bias is not None:
        in_specs.append(pl.BlockSpec((1, tn), lambda i, j: (0, j)))
        args.append(bias.reshape(1, N))
    return pl.pallas_call(
        functools.partial(_mm_body, has_bias=bias is not None),
        out_shape=jax.ShapeDtypeStruct((M, N), out_dtype),
        grid=(M // tm, N // tn),
        in_specs=in_specs,
        out_specs=pl.BlockSpec((tm, tn), lambda i, j: (i, j)),
        compiler_params=_cparams(("parallel", "arbitrary"), 48),
        name=name,
    )(*args)


def _mod_specs(mod, rows_per_seq, tm, D):
    if rows_per_seq == 1:
        return mod, pl.BlockSpec((tm, D), lambda i: (i, 0))
    assert rows_per_seq % tm == 0
    tps = rows_per_seq // tm
    return mod.reshape(mod.shape[0], 1, D), pl.BlockSpec((None, 1, D), lambda i: (i // tps, 0, 0))


def _normmod_body(x_ref, w_ref, sc_ref, sh_ref, o_ref):
    h = _rms(x_ref[...], w_ref[...])
    o_ref[...] = (h * (1.0 + sc_ref[...]) + sh_ref[...]).astype(o_ref.dtype)


def _normmod(x, w, sc, sh, rows_per_seq):
    M, D = x.shape
    tm = min(512, rows_per_seq) if rows_per_seq > 1 else M
    sc_a, spec = _mod_specs(sc, rows_per_seq, tm, D)
    sh_a, _ = _mod_specs(sh, rows_per_seq, tm, D)
    return pl.pallas_call(
        _normmod_body,
        out_shape=jax.ShapeDtypeStruct((M, D), BF16),
        grid=(M // tm,),
        in_specs=[pl.BlockSpec((tm, D), lambda i: (i, 0)), pl.BlockSpec((1, D), lambda i: (0, 0)), spec, spec],
        out_specs=pl.BlockSpec((tm, D), lambda i: (i, 0)),
        compiler_params=_cparams(("parallel",), 40),
        name="normmod",
    )(x, w.reshape(1, D), sc_a, sh_a)


def _merge_body(on_ref, od_ref, g1_ref, g2_ref, wa_ref, wb_ref, o_ref):
    a = _dot1(on_ref[...], wa_ref[...])
    b = _dot1(od_ref[...], wb_ref[...])
    o_ref[...] = (_sigmoid(g1_ref[...]) * a + _sigmoid(g2_ref[...]) * b).astype(o_ref.dtype)


def _merge(o_nsa, o_dn, mg, wa, wb):
    M, K = o_nsa.shape
    D = wa.shape[1]
    tm = min(M, 512)
    tn = 1024
    nj = D // tn
    return pl.pallas_call(
        _merge_body,
        out_shape=jax.ShapeDtypeStruct((M, D), BF16),
        grid=(M // tm, nj),
        in_specs=[pl.BlockSpec((tm, K), lambda i, j: (i, 0)), pl.BlockSpec((tm, K), lambda i, j: (i, 0)),
                  pl.BlockSpec((tm, tn), lambda i, j: (i, j)), pl.BlockSpec((tm, tn), lambda i, j: (i, j + nj)),
                  pl.BlockSpec((K, tn), lambda i, j: (0, j)), pl.BlockSpec((K, tn), lambda i, j: (0, j))],
        out_specs=pl.BlockSpec((tm, tn), lambda i, j: (i, j)),
        compiler_params=_cparams(("parallel", "arbitrary"), 40),
        name="merge",
    )(o_nsa, o_dn, mg, mg, wa, wb)


def _outproj_body(x_ref, mx_ref, w_ref, gt_ref, nw_ref, sc_ref, sh_ref, x1_o, h2_o):
    x1 = x_ref[...] + gt_ref[...] * _dot1(mx_ref[...], w_ref[...])
    x1_o[...] = x1
    h2_o[...] = (_rms(x1, nw_ref[...]) * (1.0 + sc_ref[...]) + sh_ref[...]).astype(h2_o.dtype)


def _outproj(x, mixed, w_out, gt1, norm2_w, sc2, sh2, rows_per_seq):
    M, D = x.shape
    tm = min(256, rows_per_seq) if rows_per_seq > 1 else M
    gt_a, spec = _mod_specs(gt1, rows_per_seq, tm, D)
    sc_a, _ = _mod_specs(sc2, rows_per_seq, tm, D)
    sh_a, _ = _mod_specs(sh2, rows_per_seq, tm, D)
    row = pl.BlockSpec((tm, D), lambda i: (i, 0))
    return pl.pallas_call(
        _outproj_body,
        out_shape=(jax.ShapeDtypeStruct((M, D), F32), jax.ShapeDtypeStruct((M, D), BF16)),
        grid=(M // tm,),
        in_specs=[row, row, pl.BlockSpec((D, D), lambda i: (0, 0)), spec,
                  pl.BlockSpec((1, D), lambda i: (0, 0)), spec, spec],
        out_specs=(row, row),
        compiler_params=_cparams(("parallel",), 48),
        name="outproj",
    )(x, mixed, w_out, gt_a, norm2_w.reshape(1, D), sc_a, sh_a)


def _residual_body(x_ref, p_ref, gt_ref, o_ref):
    o_ref[...] = x_ref[...] + gt_ref[...] * p_ref[...]


def _residual(x1, peer, gt2, rows_per_seq):
    M, D = x1.shape
    tm = min(512, rows_per_seq) if rows_per_seq > 1 else M
    gt_a, spec = _mod_specs(gt2, rows_per_seq, tm, D)
    row = pl.BlockSpec((tm, D), lambda i: (i, 0))
    return pl.pallas_call(
        _residual_body,
        out_shape=jax.ShapeDtypeStruct((M, D), F32),
        grid=(M // tm,),
        in_specs=[row, row, spec],
        out_specs=row,
        compiler_params=_cparams(("parallel",), 40),
        name="residual",
    )(x1, peer, gt_a)


def _qkprep_body(qkv_ref, cos_ref, sin_ref, qw_ref, kw_ref,
                 q_o, kc_o, vc_o, ks_o, vs_o, kwn_o, vw_o, ksb_o, vsb_o, kwb_o, vwb_o):
    cos = cos_ref[...]
    sin = sin_ref[...]
    for h in range(NSA_HEADS):
        sl = slice(h * HEAD_DIM, (h + 1) * HEAD_DIM)
        q_o[:, sl] = _rope(_rms(qkv_ref[:, sl], qw_ref[...]), cos, sin)
    base = NSA_HEADS * HEAD_DIM
    gw = NSA_KV * HEAD_DIM
    for g in range(NSA_KV):
        sl = slice(g * HEAD_DIM, (g + 1) * HEAD_DIM)

        def col(part):
            return qkv_ref[:, base + part * gw + g * HEAD_DIM: base + part * gw + (g + 1) * HEAD_DIM]

        kc_o[:, sl] = _rms(col(0), kw_ref[0:1, :])
        vc_o[:, sl] = col(1)
        ks = _rope(_rms(col(2), kw_ref[1:2, :]), cos, sin)
        ks_o[:, sl] = ks
        ksb_o[:, sl] = ks.astype(BF16)
        vs = col(3)
        vs_o[:, sl] = vs
        vsb_o[:, sl] = vs.astype(BF16)
        kw = _rope(_rms(col(4), kw_ref[2:3, :]), cos, sin)
        kwn_o[:, sl] = kw
        kwb_o[:, sl] = kw.astype(BF16)
        vw = col(5)
        vw_o[:, sl] = vw
        vwb_o[:, sl] = vw.astype(BF16)


def _qkprep(qkv, cos, sin, q_norm_w, k_norm_w, rows_per_seq):
    M, W = qkv.shape
    tm = min(256, rows_per_seq) if rows_per_seq > 1 else M
    tps = max(rows_per_seq // tm, 1)
    gw = NSA_KV * HEAD_DIM
    row = lambda w: pl.BlockSpec((tm, w), lambda i: (i, 0))
    tab = pl.BlockSpec((tm, HEAD_DIM), lambda i: (i % tps, 0))
    shapes = [jax.ShapeDtypeStruct((M, NSA_HEADS * HEAD_DIM), F32)] + [jax.ShapeDtypeStruct((M, gw), F32)] * 6 \
        + [jax.ShapeDtypeStruct((M, gw), BF16)] * 4
    return pl.pallas_call(
        _qkprep_body,
        out_shape=tuple(shapes),
        grid=(M // tm,),
        in_specs=[row(W), tab, tab, pl.BlockSpec((1, HEAD_DIM), lambda i: (0, 0)),
                  pl.BlockSpec((3, HEAD_DIM), lambda i: (0, 0))],
        out_specs=tuple([row(NSA_HEADS * HEAD_DIM)] + [row(gw)] * 10),
        compiler_params=_cparams(("parallel",), 40),
        name="qkprep",
    )(qkv, cos, sin, q_norm_w.reshape(1, HEAD_DIM), k_norm_w)


def _rope_tables(pos):
    half = HEAD_DIM // 2
    inv = ROPE_THETA ** (-jnp.arange(half, dtype=F32) / half)
    ang = jnp.asarray(pos, F32)[:, None] * inv
    cos, sin = jnp.cos(ang), jnp.sin(ang)
    return jnp.concatenate([cos, cos], axis=-1), jnp.concatenate([-sin, sin], axis=-1)


def _compress_chunks(x, w_ref, n):
    xr = x.reshape(n, CMP_STRIDE, x.shape[-1])
    first = jnp.sum(xr * w_ref[0:CMP_STRIDE, :][None], axis=1)
    second = jnp.sum(xr * w_ref[CMP_STRIDE:CMP_LEN, :][None], axis=1)
    return first, second


def _finish_compress(first, second, b_ref, cos_ref, sin_ref, rope):
    n = first.shape[0]
    c = first + pltpu.roll(second, n - 1, 0) + b_ref[...]
    if not rope:
        return c
    parts = [_rope(c[:, g * HEAD_DIM:(g + 1) * HEAD_DIM], cos_ref[...], sin_ref[...]) for g in range(NSA_KV)]
    return jnp.concatenate(parts, axis=-1)


def _compress_body(x_ref, w_ref, b_ref, cos_ref, sin_ref, o_ref, *, rope):
    n = x_ref.shape[0] // CMP_STRIDE
    first, second = _compress_chunks(x_ref[...], w_ref, n)
    o_ref[...] = _finish_compress(first, second, b_ref, cos_ref, sin_ref, rope)


def _compress_prompt(x, w, b, cos_c, sin_c, rope):
    B, T, W = x.shape
    n = T // CMP_STRIDE
    full = lambda s: pl.BlockSpec(s, lambda i: (0,) * len(s))
    return pl.pallas_call(
        functools.partial(_compress_body, rope=rope),
        out_shape=jax.ShapeDtypeStruct((B, n, W), F32),
        grid=(B,),
        in_specs=[pl.BlockSpec((None, T, W), lambda i: (i, 0, 0)), full((CMP_LEN, W)), full((1, W)),
                  full((n, HEAD_DIM)), full((n, HEAD_DIM))],
        out_specs=pl.BlockSpec((None, n, W), lambda i: (i, 0, 0)),
        compiler_params=_cparams(("parallel",), 40),
        name="compress",
    )(x, w, b, cos_c, sin_c)


def _overlap_matrix(n_rows, n_cmp, n_cols, n_slc):
    cs = np.arange(n_rows)[:, None] * CMP_STRIDE
    ss = np.arange(n_cols)[None, :] * SLC_LEN
    m = (cs < ss + SLC_LEN) & (cs + CMP_LEN > ss)
    m &= (np.arange(n_rows)[:, None] < n_cmp) & (np.arange(n_cols)[None, :] < n_slc)
    return jnp.asarray(m.astype(np.float32), BF16)


def _nsa_prompt_body(q_ref, gate_ref, ck_ref, cv_ref, ks_ref, vs_ref, kw_ref, vw_ref, ov_ref, ex_ref, o_ref,
                     *, tq, T, n_cmp, n_slc, n_sel, span):
    t0 = pl.program_id(2) * tq
    scale = HEAD_DIM ** -0.5
    qpos = t0 + lax.broadcasted_iota(jnp.int32, (tq, 1), 0)
    qs = [q_ref[:, r * HEAD_DIM:(r + 1) * HEAD_DIM] for r in range(NSA_REP)]

    ncp = ck_ref.shape[0]
    nidx = lax.broadcasted_iota(jnp.int32, (1, ncp), 1)
    cmask = ((nidx * CMP_STRIDE + CMP_LEN - 1) <= qpos) & (nidx < n_cmp)
    ck = ck_ref[...]
    cv = cv_ref[...].astype(BF16)
    p_sum = jnp.zeros((tq, ncp), F32)
    o_cmp = []
    for r in range(NSA_REP):
        p = _masked_softmax_rows(_dot3(qs[r], ck, 1, 1) * scale, cmask)
        p_sum = p_sum + p
        o_cmp.append(_dot1(p, cv))

    imp = _dot_sel_rhs(p_sum, ov_ref[...])
    blk = lax.broadcasted_iota(jnp.int32, (1, n_slc), 1)
    cur = qpos // SLC_LEN
    visible = blk <= cur
    forced = (blk == 0) | (blk == cur) | (blk == cur - 1)
    score = jnp.where(forced, 1e9, jnp.where(visible, imp, -1e9))
    rank = _rank_desc(score, blk, n_slc)
    sel = jnp.where((rank < n_sel) & visible, 1.0, 0.0)
    kmask = _dot1(sel, ex_ref[...])

    kpos = lax.broadcasted_iota(jnp.int32, (1, T), 1)
    allow = (kmask > 0.5) & (kpos <= qpos)
    ks = ks_ref[...]
    vs = vs_ref[...]
    o_slc = []
    for r in range(NSA_REP):
        s = jnp.where(allow, _dot1(qs[r], ks, 1, 1) * scale, NEG)
        e = jnp.exp(s - jnp.max(s, axis=-1, keepdims=True))
        o_slc.append(_dot1(e, vs) / jnp.sum(e, axis=-1, keepdims=True))

    kstart = pl.multiple_of(jnp.clip(t0 - WINDOW, 0, T - span), tq)
    kw = kw_ref[pl.ds(kstart, span), :]
    vw = vw_ref[pl.ds(kstart, span), :]
    kposw = kstart + lax.broadcasted_iota(jnp.int32, (1, span), 1)
    allow_w = (kposw <= qpos) & (kposw >= qpos - WINDOW)
    gt = _sigmoid(gate_ref[...])
    for r in range(NSA_REP):
        s = jnp.where(allow_w, _dot1(qs[r], kw, 1, 1) * scale, NEG)
        e = jnp.exp(s - jnp.max(s, axis=-1, keepdims=True))
        o_win = _dot1(e, vw) / jnp.sum(e, axis=-1, keepdims=True)
        o = (gt[:, 3 * r:3 * r + 1] * o_cmp[r] + gt[:, 3 * r + 1:3 * r + 2] * o_slc[r]
             + gt[:, 3 * r + 2:3 * r + 3] * o_win)
        o_ref[:, r * HEAD_DIM:(r + 1) * HEAD_DIM] = o.astype(o_ref.dtype)


def _nsa_prompt(q, gate_g, ck, cv, ksb, vsb, kwb, vwb, B, T):
    tq = 128
    nq = T // tq
    n_cmp = (T - CMP_LEN) // CMP_STRIDE + 1
    ncp = T // CMP_STRIDE
    n_slc = -(-T // SLC_LEN)
    n_sel = min(SLC_TOPN, n_slc)
    span = min(WINDOW + tq, T)
    gw = NSA_REP * HEAD_DIM
    ov = _overlap_matrix(ncp, n_cmp, n_slc, n_slc)
    ex = jnp.asarray((np.arange(T)[None, :] // SLC_LEN == np.arange(n_slc)[:, None]).astype(np.float32), BF16)
    seq = lambda: pl.BlockSpec((None, T, HEAD_DIM), lambda b, g, i: (b, 0, g))
    cmp_spec = lambda: pl.BlockSpec((None, ncp, HEAD_DIM), lambda b, g, i: (b, 0, g))
    body = functools.partial(_nsa_prompt_body, tq=tq, T=T, n_cmp=n_cmp, n_slc=n_slc, n_sel=n_sel, span=span)
    return pl.pallas_call(
        body,
        out_shape=jax.ShapeDtypeStruct((B * T, NSA_HEADS * HEAD_DIM), BF16),
        grid=(B, NSA_KV, nq),
        in_specs=[pl.BlockSpec((tq, gw), lambda b, g, i: (b * nq + i, g)),
                  pl.BlockSpec((None, tq, 3 * NSA_REP), lambda b, g, i: (g, b * nq + i, 0)),
                  cmp_spec(), cmp_spec(), seq(), seq(), seq(), seq(),
                  pl.BlockSpec((ncp, n_slc), lambda b, g, i: (0, 0)),
                  pl.BlockSpec((n_slc, T), lambda b, g, i: (0, 0))],
        out_specs=pl.BlockSpec((tq, gw), lambda b, g, i: (b * nq + i, g)),
        compiler_params=_cparams(("parallel", "parallel", "arbitrary"), 48),
        name="nsa_prompt",
    )(q, gate_g, ck, cv, ksb.reshape(B, T, -1), vsb.reshape(B, T, -1), kwb.reshape(B, T, -1),
      vwb.reshape(B, T, -1), ov, ex)


def _interleaved_cmp_weights(w):
    out = []
    for g in range(NSA_KV):
        for half in range(2):
            wh = w[half * CMP_STRIDE:(half + 1) * CMP_STRIDE]
            z = jnp.zeros_like(wh)
            parts = [wh if gg == g else z for gg in range(NSA_KV)]
            out.append(jnp.stack(parts, axis=1).reshape(CMP_STRIDE * NSA_KV, w.shape[1]))
    return jnp.stack(out)


def _nsa_dec1_body(pt_ref, q_ref, *refs, n_cmp, n_slc, qpos):
    PG = PAGE_GROUP
    kp, vp = refs[:PG], refs[PG:2 * PG]
    wk, bk, wv, bv, cos_ref, sin_ref, ov_ref = refs[2 * PG:2 * PG + 7]
    ocmp_o, idx_o = refs[2 * PG + 7:2 * PG + 9]
    fk, sk, fv, sv = refs[2 * PG + 9:]
    pg = pl.program_id(1)
    chunk = CMP_STRIDE * NSA_KV
    rows_pp = kp[0].shape[0] // chunk
    for p in range(PG):
        base = pl.multiple_of((pg * PG + p) * rows_pp, rows_pp)
        xk = kp[p][...].reshape(rows_pp, chunk, HEAD_DIM)
        xv = vp[p][...].reshape(rows_pp, chunk, HEAD_DIM)
        for g in range(NSA_KV):
            sl = slice(g * HEAD_DIM, (g + 1) * HEAD_DIM)
            fk[pl.ds(base, rows_pp), sl] = jnp.sum(xk * wk[2 * g][None], axis=1)
            sk[pl.ds(base, rows_pp), sl] = jnp.sum(xk * wk[2 * g + 1][None], axis=1)
            fv[pl.ds(base, rows_pp), sl] = jnp.sum(xv * wv[2 * g][None], axis=1)
            sv[pl.ds(base, rows_pp), sl] = jnp.sum(xv * wv[2 * g + 1][None], axis=1)

    @pl.when(pg == pl.num_programs(1) - 1)
    def _():
        scale = HEAD_DIM ** -0.5
        ncp = fk.shape[0]
        ck = _finish_compress(fk[...], sk[...], bk, cos_ref, sin_ref, True)
        cv = _finish_compress(fv[...], sv[...], bv, cos_ref, sin_ref, False)
        q8 = q_ref[...]
        nidx = lax.broadcasted_iota(jnp.int32, (1, ncp), 1)
        cmask = ((nidx * CMP_STRIDE + CMP_LEN - 1) <= qpos) & (nidx < n_cmp)
        rowi = lax.broadcasted_iota(jnp.int32, (NSA_HEADS, 1), 0)
        ocmp = jnp.zeros((NSA_HEADS, HEAD_DIM), F32)
        p2 = jnp.zeros((NSA_HEADS, ncp), F32)
        for g in range(NSA_KV):
            sl = slice(g * HEAD_DIM, (g + 1) * HEAD_DIM)
            p = _masked_softmax_rows(_dot3(q8, ck[:, sl], 1, 1) * scale, cmask)
            in_g = (rowi >= g * NSA_REP) & (rowi < (g + 1) * NSA_REP)
            ocmp = jnp.where(in_g, _dot1(p, cv[:, sl]), ocmp)
            ps = jnp.sum(jnp.where(in_g, p, 0.0), axis=0, keepdims=True)
            p2 = jnp.where(rowi == g, ps, p2)
        ocmp_o[...] = ocmp

        imp = _dot_sel_rhs(p2, ov_ref[...])
        nsp = ov_ref.shape[1]
        blk = lax.broadcasted_iota(jnp.int32, (1, nsp), 1)
        cur = qpos // SLC_LEN
        forced = (blk == 0) | (blk == cur) | (blk == cur - 1)
        score = jnp.where(forced, 1e9, jnp.where(blk <= cur, imp, -1e9))
        score = jnp.where(blk < n_slc, score, -2e9)
        rank = _rank_desc(score, blk, n_slc)
        kcol = lax.broadcasted_iota(jnp.int32, (SLC_TOPN, 1), 0).astype(F32)
        blkf = blk.astype(F32)
        for g in range(NSA_KV):
            hit = rank[g:g + 1, :] == kcol
            ids = jnp.sum(jnp.where(hit, blkf, 0.0), axis=1, keepdims=True)
            idx_o[g] = jnp.broadcast_to(ids, (SLC_TOPN, HEAD_DIM)).astype(jnp.int32)


def _nsa_dec1(q8, page_table, cache_k, cache_v, wk, bk, wv, bv, cos_c, sin_c, past_len):
    DB, n_pages = page_table.shape
    PG = PAGE_GROUP
    assert n_pages % PG == 0
    prow = cache_k.shape[1]
    W = NSA_KV * HEAD_DIM
    ncp = past_len // CMP_STRIDE
    n_cmp = (past_len + 1 - CMP_LEN) // CMP_STRIDE + 1
    n_slc = -(-(past_len + 1) // SLC_LEN)
    nsp = -(-n_slc // 128) * 128
    assert past_len // SLC_LEN + 1 == n_slc and n_slc >= SLC_TOPN
    ov = _overlap_matrix(ncp, n_cmp, nsp, n_slc)

    def page_spec(p):
        return pl.BlockSpec((None, prow, HEAD_DIM), lambda b, pg, pt: (pt[b, pg * PG + p], 0, 0))

    full = lambda s: pl.BlockSpec(s, lambda b, pg, pt: (0,) * len(s))
    in_specs = [pl.BlockSpec((None, NSA_HEADS, HEAD_DIM), lambda b, pg, pt: (b, 0, 0))]
    in_specs += [page_spec(p) for p in range(PG)] + [page_spec(p) for p in range(PG)]
    in_specs += [full(wk.shape), full((1, W)), full(wv.shape), full((1, W)),
                 full((ncp, HEAD_DIM)), full((ncp, HEAD_DIM)), full((ncp, nsp))]
    grid_spec = pltpu.PrefetchScalarGridSpec(
        num_scalar_prefetch=1, grid=(DB, n_pages // PG), in_specs=in_specs,
        out_specs=(pl.BlockSpec((None, NSA_HEADS, HEAD_DIM), lambda b, pg, pt: (b, 0, 0)),
                   pl.BlockSpec((None, NSA_KV, SLC_TOPN, HEAD_DIM), lambda b, pg, pt: (b, 0, 0, 0))),
        scratch_shapes=[pltpu.VMEM((ncp, W), F32)] * 4)
    body = functools.partial(_nsa_dec1_body, n_cmp=n_cmp, n_slc=n_slc, qpos=past_len)
    return pl.pallas_call(
        body,
        out_shape=(jax.ShapeDtypeStruct((DB, NSA_HEADS, HEAD_DIM), F32),
                   jax.ShapeDtypeStruct((DB, NSA_KV, SLC_TOPN, HEAD_DIM), jnp.int32)),
        grid_spec=grid_spec,
        compiler_params=_cparams(("parallel", "arbitrary"), 40),
        name="nsa_dec_cmp",
    )(page_table, q8, *([cache_k] * PG), *([cache_v] * PG), wk, bk, wv, bv, cos_c, sin_c, ov)


def _attend_rows(q, g, keys, k_new, vals, v_new, valid, scale):
    qb = q.astype(BF16)
    s_new = jnp.sum(q * k_new, axis=-1, keepdims=True) * scale
    s_blk = []
    m = s_new
    for kk, ok in zip(keys, valid):
        s = _dg(qb, kk[...].astype(BF16), 1, 1) * scale
        own = (lax.broadcasted_iota(jnp.int32, (1, s.shape[1]), 1) & (NSA_KV - 1)) == g
        s = jnp.where(own if ok is None else own & ok, s, NEG)
        s_blk.append(s)
        m = jnp.maximum(m, jnp.max(s, axis=-1, keepdims=True))
    e_new = jnp.exp(s_new - m)
    den = e_new
    acc = e_new * v_new
    for s, vv in zip(s_blk, vals):
        e = jnp.exp(s - m)
        den = den + jnp.sum(e, axis=-1, keepdims=True)
        acc = acc + _dot1(e, vv[...])
    return acc / den


def _shift_in(buf, new_rows):
    n, G = buf.shape[0], new_rows.shape[0]
    out = pltpu.roll(buf, n - G, 0)
    rowi = lax.broadcasted_iota(jnp.int32, (n, 1), 0)
    for g in range(G):
        out = jnp.where(rowi == n - G + g, new_rows[g:g + 1, :], out)
    return out


def _nsa_dec2_body(pt_ref, idx_ref, q_ref, gate_ref, ocmp_ref, *refs, n_slc):
    n = SLC_TOPN
    kb, vb = refs[:n], refs[n:2 * n]
    kn, vn, wk, wv, wkn, wvn = refs[2 * n:2 * n + 6]
    o_ref, wko, wvo = refs[2 * n + 6:]
    b = pl.program_id(0)
    g = pl.program_id(1)
    scale = HEAD_DIM ** -0.5
    q = q_ref[...]
    valid = [idx_ref[(b * NSA_KV + g) * n + k] < n_slc - 1 for k in range(n)]
    own = pl.ds(g, 1)
    o_slc = _attend_rows(q, g, kb, kn[own, :], vb, vn[own, :], valid, scale)
    o_win = _attend_rows(q, g, [wk], wkn[own, :], [wv], wvn[own, :], [None], scale)
    gt = _sigmoid(gate_ref[...])
    o_ref[...] = gt[:, 0:1] * ocmp_ref[...] + gt[:, 1:2] * o_slc + gt[:, 2:3] * o_win

    @pl.when(g == 0)
    def _():
        wko[...] = _shift_in(wk[...], wkn[...])
        wvo[...] = _shift_in(wv[...], wvn[...])


def _nsa_dec2(q4, gate4, ocmp4, page_table, idx, slc_k, slc_v, k_new, v_new, win_k, win_v, wk_new, wv_new,
              past_len):
    DB = q4.shape[0]
    n = SLC_TOPN
    n_slc = -(-(past_len + 1) // SLC_LEN)
    brow = SLC_LEN * NSA_KV
    per_page = slc_k.shape[1] // brow
    hk = slc_k.reshape(slc_k.shape[0] * per_page, brow, HEAD_DIM)
    hv = slc_v.reshape(slc_v.shape[0] * per_page, brow, HEAD_DIM)
    wrow = win_k.shape[1]

    def blk_spec(k):
        def imap(b, g, pt, ix):
            s = jnp.minimum(ix[(b * NSA_KV + g) * n + k], n_slc - 2)
            return (pt[b, s // per_page] * per_page + s % per_page, 0, 0)
        return pl.BlockSpec((None, brow, HEAD_DIM), imap)

    head = lambda w: pl.BlockSpec((None, None, NSA_REP, w), lambda b, g, pt, ix: (b, g, 0, 0))
    new = lambda: pl.BlockSpec((None, NSA_KV, HEAD_DIM), lambda b, g, pt, ix: (b, 0, 0))
    wspec = lambda: pl.BlockSpec((None, wrow, HEAD_DIM), lambda b, g, pt, ix: (b, 0, 0))
    blocks = lambda: [blk_spec(k) for k in range(n)]
    in_specs = [head(HEAD_DIM), head(3), head(HEAD_DIM)] + blocks() + blocks() \
        + [new(), new(), wspec(), wspec(), new(), new()]
    grid_spec = pltpu.PrefetchScalarGridSpec(
        num_scalar_prefetch=2, grid=(DB, NSA_KV), in_specs=in_specs,
        out_specs=(head(HEAD_DIM), wspec(), wspec()))
    return pl.pallas_call(
        functools.partial(_nsa_dec2_body, n_slc=n_slc),
        out_shape=(jax.ShapeDtypeStruct(q4.shape, F32),
                   jax.ShapeDtypeStruct(win_k.shape, F32), jax.ShapeDtypeStruct(win_v.shape, F32)),
        grid_spec=grid_spec,
        compiler_params=_cparams(("parallel", "arbitrary"), 40),
        name="nsa_dec_attend",
    )(page_table, idx.reshape(-1), q4, gate4, ocmp4, *([hk] * n), *([hv] * n),
      k_new, v_new, win_k, win_v, wk_new, wv_new)


def _dn_gates(sm, al_ref, dtb_ref):
    g = -jnp.exp(al_ref[...]) * _softplus(sm + dtb_ref[...])
    return g, _sigmoid(sm)


def _dn_out(o, z, nw_ref):
    return _rms(o, nw_ref[...]) * _silu(z)


def _dn_prompt_body(x_ref, z_ref, sm_ref, aT_ref, cw_ref, al_ref, alT_ref, dtb_ref, dtbT_ref, nw_ref,
                    o_ref, s_out_ref, xbuf, S):
    C = DN_CHUNK
    c = pl.program_id(1)

    @pl.when(c == 0)
    def _():
        xbuf[0:8, :] = jnp.zeros((8, xbuf.shape[1]), F32)
        S[...] = jnp.zeros(S.shape, F32)

    xbuf[8:8 + C, :] = x_ref[...]
    lo = 8 - (DN_CONV - 1)
    y = cw_ref[0:1, :] * xbuf[lo:lo + C, :]
    for j in range(1, DN_CONV):
        y = y + cw_ref[j:j + 1, :] * xbuf[lo + j:lo + j + C, :]
    xbuf[0:8, :] = xbuf[C:C + 8, :]
    y = _silu(y)

    g, beta = _dn_gates(sm_ref[...], al_ref, dtb_ref)
    gT = -jnp.exp(alT_ref[...]) * _softplus(aT_ref[...] + dtbT_ref[...])
    row = lax.broadcasted_iota(jnp.int32, (C, C), 0)
    col = lax.broadcasted_iota(jnp.int32, (C, C), 1)
    tril = row >= col
    strict = row > col
    G = _dot_sel_lhs(jnp.where(tril, 1.0, 0.0).astype(BF16), g)
    GT = _dot_sel_rhs(gT, jnp.where(row <= col, 1.0, 0.0).astype(BF16))
    eye = jnp.where(row == col, 1.0, 0.0)
    pair = jnp.right_shift(row, 1) == jnp.right_shift(col, 1)
    merges = []
    for lvl in range(1, int(np.log2(C))):
        merges.append((jnp.right_shift(row, lvl + 1) == jnp.right_shift(col, lvl + 1))
                      & (jnp.right_shift(row, lvl) != jnp.right_shift(col, lvl)))
    nh = DN_HEADS
    hs = range(nh)
    q = [_l2(y[:, h * DN_DK:(h + 1) * DN_DK]) for h in hs]
    k = [_l2(y[:, (nh + h) * DN_DK:(nh + h + 1) * DN_DK]) for h in hs]
    v = [y[:, 2 * nh * DN_DK + h * DN_DV:2 * nh * DN_DK + (h + 1) * DN_DV] for h in hs]
    Gc = [G[:, DNA_COL + h:DNA_COL + h + 1] for h in hs]
    bc = [beta[:, DNB_COL + h:DNB_COL + h + 1] for h in hs]
    decay = [jnp.where(tril, jnp.exp(jnp.where(tril, Gc[h] - GT[h:h + 1, :], 0.0)), 0.0) for h in hs]
    qc = [q[h] * (DN_DK ** -0.5) for h in hs]
    kb = [k[h] * bc[h] for h in hs]
    M = [jnp.where(strict, _dot1(kb[h], k[h], 1, 1) * decay[h], 0.0) for h in hs]
    P = [eye - jnp.where(pair, M[h], 0.0) for h in hs]
    for off in merges:
        W = [_dot3(jnp.where(off, M[h], 0.0), P[h]) for h in hs]
        P = [P[h] - _dot3(P[h], W[h]) for h in hs]
    eG = [jnp.exp(Gc[h]) for h in hs]
    uw = [_dot3(P[h], jnp.concatenate([v[h] * bc[h], kb[h] * eG[h]], axis=1)) for h in hs]
    aqk = [jnp.where(tril, _dot1(qc[h], k[h], 1, 1) * decay[h], 0.0) for h in hs]
    Sh = [S[h] for h in hs]
    v_new = [uw[h][:, :DN_DV] - _dot1(uw[h][:, DN_DV:], Sh[h]) for h in hs]
    o = [_dot1(qc[h] * eG[h], Sh[h]) + _dot1(aqk[h], v_new[h]) for h in hs]
    for h in hs:
        Gl = Gc[h][C - 1:C, :]
        kdec = k[h] * jnp.exp(Gl - Gc[h])
        S[h] = Sh[h] * jnp.exp(Gl) + _dot1(kdec.T, v_new[h])
        zs = slice(h * DN_DV, (h + 1) * DN_DV)
        o_ref[:, zs] = _dn_out(o[h], z_ref[:, zs], nw_ref).astype(o_ref.dtype)

    @pl.when(c == pl.num_programs(1) - 1)
    def _():
        s_out_ref[...] = S[...]


def _dn_gate_rows(a_log, dt_bias):
    pad = lambda v: jnp.zeros((1, 128), F32).at[0, DNA_COL:DNA_COL + DN_HEADS].set(v)
    return pad(a_log), pad(dt_bias)


def _dn_prompt(dnx, z, small, conv_w, a_log, dt_bias, norm_w, B, T):
    C = DN_CHUNK
    assert T % C == 0
    nc = T // C
    Wx = dnx.shape[1]
    Wz = z.shape[1]
    aT = small[:, DNA_COL:DNA_COL + DN_HEADS].reshape(B, T, DN_HEADS).transpose(0, 2, 1)
    al, dtb = _dn_gate_rows(a_log, dt_bias)
    full = lambda s: pl.BlockSpec(s, lambda b, c: (0,) * len(s))
    row = lambda w: pl.BlockSpec((C, w), lambda b, c: (b * nc + c, 0))
    return pl.pallas_call(
        _dn_prompt_body,
        out_shape=(jax.ShapeDtypeStruct((B * T, Wz), BF16),
                   jax.ShapeDtypeStruct((B, DN_HEADS, DN_DK, DN_DV), F32)),
        grid=(B, nc),
        in_specs=[row(Wx), row(Wz), row(128), pl.BlockSpec((None, DN_HEADS, C), lambda b, c: (b, 0, c)),
                  full((DN_CONV, Wx)), full((1, 128)), full((DN_HEADS, 1)), full((1, 128)), full((DN_HEADS, 1)),
                  full((1, DN_DV))],
        out_specs=(row(Wz), pl.BlockSpec((None, DN_HEADS, DN_DK, DN_DV), lambda b, c: (b, 0, 0, 0))),
        scratch_shapes=[pltpu.VMEM((C + 8, Wx), F32), pltpu.VMEM((DN_HEADS, DN_DK, DN_DV), F32)],
        compiler_params=_cparams(("parallel", "arbitrary"), 48),
        name="deltanet_prompt",
    )(dnx, z, small, aT, conv_w, al, a_log.reshape(DN_HEADS, 1), dtb, dt_bias.reshape(DN_HEADS, 1),
      norm_w.reshape(1, DN_DV))


def _dn_dec_body(x_ref, hist_ref, sm_ref, z_ref, s0_ref, cw_ref, al_ref, dtb_ref, nw_ref,
                 o_ref, conv_o, s_o):
    x = x_ref[...]
    hist = hist_ref[...]
    y = cw_ref[DN_CONV - 1:DN_CONV, :] * x
    for j in range(DN_CONV - 1):
        y = y + cw_ref[j:j + 1, :] * hist[j:j + 1, :]
    conv_o[0:DN_CONV - 2, :] = hist[1:DN_CONV - 1, :]
    conv_o[DN_CONV - 2:DN_CONV - 1, :] = x
    y = _silu(y)
    g, beta = _dn_gates(sm_ref[...], al_ref, dtb_ref)
    n = DN_DK
    diag = lax.broadcasted_iota(jnp.int32, (n, n), 0) == lax.broadcasted_iota(jnp.int32, (n, n), 1)

    def column(r):
        return jnp.sum(jnp.where(diag, jnp.broadcast_to(r, (n, n)), 0.0), axis=1, keepdims=True)

    nh = DN_HEADS
    scale = DN_DK ** -0.5
    for h in range(nh):
        q = _l2(y[:, h * DN_DK:(h + 1) * DN_DK])
        k = _l2(y[:, (nh + h) * DN_DK:(nh + h + 1) * DN_DK])
        v = y[:, 2 * nh * DN_DK + h * DN_DV:2 * nh * DN_DK + (h + 1) * DN_DV]
        eg = jnp.exp(g[:, DNA_COL + h:DNA_COL + h + 1])
        bt = beta[:, DNB_COL + h:DNB_COL + h + 1]
        kcol = column(k)
        Sh = s0_ref[h]
        kS = jnp.sum(Sh * kcol, axis=0, keepdims=True)
        qS = jnp.sum(Sh * column(q), axis=0, keepdims=True)
        v_new = bt * (v - eg * kS)
        o = scale * (eg * qS + jnp.sum(q * k, axis=-1, keepdims=True) * v_new)
        s_o[h] = Sh * eg + kcol * v_new
        o_ref[:, h * DN_DV:(h + 1) * DN_DV] = _dn_out(o, z_ref[:, h * DN_DV:(h + 1) * DN_DV], nw_ref).astype(o_ref.dtype)


def _dn_decode(dnx, hist, small, z, s0, conv_w, a_log, dt_bias, norm_w):
    DB, Wx = dnx.shape
    Wz = z.shape[1]
    al, dtb = _dn_gate_rows(a_log, dt_bias)
    full = lambda s: pl.BlockSpec(s, lambda b: (0,) * len(s))
    one = lambda w: pl.BlockSpec((None, 1, w), lambda b: (b, 0, 0))
    nh = DN_CONV - 1
    st = pl.BlockSpec((None, DN_HEADS, DN_DK, DN_DV), lambda b: (b, 0, 0, 0))
    return pl.pallas_call(
        _dn_dec_body,
        out_shape=(jax.ShapeDtypeStruct((DB, 1, Wz), BF16), jax.ShapeDtypeStruct((DB, nh, Wx), F32),
                   jax.ShapeDtypeStruct(s0.shape, F32)),
        grid=(DB,),
        in_specs=[one(Wx), pl.BlockSpec((None, nh, Wx), lambda b: (b, 0, 0)), one(128), one(Wz), st,
                  full((DN_CONV, Wx)), full((1, 128)), full((1, 128)), full((1, DN_DV))],
        out_specs=(one(Wz), pl.BlockSpec((None, nh, Wx), lambda b: (b, 0, 0)), st),
        compiler_params=_cparams(("parallel",), 40),
        name="deltanet_decode",
    )(dnx.reshape(DB, 1, Wx), hist, small.reshape(DB, 1, 128), z.reshape(DB, 1, Wz), s0, conv_w, al, dtb,
      norm_w.reshape(1, DN_DV))


def _top_ranks(x, rows, n_rows):
    rank = jnp.full(x.shape, float(PEER_TOPK), F32)
    vals = []
    for k in range(PEER_TOPK):
        m = jnp.max(x, axis=0, keepdims=True)
        first = jnp.min(jnp.where(x == m, rows, n_rows), axis=0, keepdims=True)
        hit = rows == first
        rank = jnp.where(hit, float(k), rank)
        x = jnp.where(hit, -jnp.inf, x)
        vals.append(m)
    return jnp.concatenate(vals, axis=0), rank


def _peer_pairs():
    K = PEER_TOPK
    pairs = [(a, b) for a in range(K) for b in range(K) if (a + 1) * (b + 1) <= K]
    n_pad = -(-len(pairs) // 16) * 16
    sel_a = np.zeros((n_pad, K), np.float32)
    sel_b = np.zeros((n_pad, K), np.float32)
    for r, (a, b) in enumerate(pairs):
        sel_a[r, a] = 1.0
        sel_b[r, b] = 1.0
    return len(pairs), sel_a, sel_b


def _peer_route_body(qh_ref, keys_ref, sa_ref, sb_ref, sat_ref, r2_o, e2_o, c1_o, e1z_o, *, n_pairs):
    tr = qh_ref.shape[0]
    K = PEER_TOPK
    half = PEER_DKEY // 2
    rows = lax.broadcasted_iota(jnp.int32, (PEER_NKEYS, tr), 0)
    n_cand = sa_ref.shape[0]
    crow = lax.broadcasted_iota(jnp.int32, (n_cand, tr), 0)
    sel_a, sel_b = sa_ref[...], sb_ref[...]
    for h in range(PEER_HEADS):
        s1 = _dot3(keys_ref[h, 0], qh_ref[:, h * PEER_DKEY:h * PEER_DKEY + half], 1, 1)
        s2 = _dot3(keys_ref[h, 1], qh_ref[:, h * PEER_DKEY + half:(h + 1) * PEER_DKEY], 1, 1)
        v1, r1 = _top_ranks(s1, rows, PEER_NKEYS)
        v2, r2 = _top_ranks(s2, rows, PEER_NKEYS)
        cand = _dot_sel_lhs(sel_a, v1) + _dot_sel_lhs(sel_b, v2)
        cand = jnp.where(crow < n_pairs, cand, -jnp.inf)
        wmat = _dot_sel_lhs(sel_a, jnp.exp(v1 - v1[0:1, :])) * _dot_sel_lhs(sel_b, jnp.exp(v2 - v2[0:1, :]))
        chosen = jnp.zeros(cand.shape, F32)
        for _ in range(K):
            m = jnp.max(cand, axis=0, keepdims=True)
            first = jnp.min(jnp.where(cand == m, crow, n_cand), axis=0, keepdims=True)
            hit = crow == first
            chosen = jnp.where(hit, 1.0, chosen)
            cand = jnp.where(hit, -jnp.inf, cand)
        z = jnp.sum(chosen * wmat, axis=0, keepdims=True)
        cnt = _dot1(sat_ref[...], chosen)
        c1 = jnp.zeros(s1.shape, F32)
        for a in range(K):
            c1 = jnp.where(r1 == float(a), cnt[a:a + 1, :], c1)
        r2_o[h] = r2.astype(r2_o.dtype)
        e2_o[h] = jnp.exp(s2 - v2[0:1, :]).astype(e2_o.dtype)
        c1_o[h] = c1
        e1z_o[h] = jnp.exp(s1 - v1[0:1, :]) / z


def _peer_route(qh, keys):
    N = qh.shape[0]
    tr = min(N, 256)
    n_pairs, sel_a, sel_b = _peer_pairs()
    spec = pl.BlockSpec((PEER_HEADS, PEER_NKEYS, tr), lambda i: (0, 0, i))
    shp = lambda dt: jax.ShapeDtypeStruct((PEER_HEADS, PEER_NKEYS, N), dt)
    full = lambda a: pl.BlockSpec(a.shape, lambda i: (0,) * a.ndim)
    consts = [jnp.asarray(sel_a, BF16), jnp.asarray(sel_b, BF16), jnp.asarray(sel_a.T, BF16)]
    return pl.pallas_call(
        functools.partial(_peer_route_body, n_pairs=n_pairs),
        out_shape=(shp(BF16), shp(BF16), shp(F32), shp(F32)),
        grid=(N // tr,),
        in_specs=[pl.BlockSpec((tr, qh.shape[1]), lambda i: (i, 0)), full(keys)] + [full(c) for c in consts],
        out_specs=(spec, spec, spec, spec),
        compiler_params=_cparams(("parallel",), 40),
        name="peer_route",
    )(qh, keys, *consts)


def _peer_dense_body(h_ref, u_ref, v_ref, r2_ref, e2_ref, c1_ref, e1z_ref, o_ref, acc):
    j = pl.program_id(1)
    te = u_ref.shape[0]
    groups = te // PEER_NKEYS

    @pl.when(j == 0)
    def _():
        acc[...] = jnp.zeros(acc.shape, F32)

    act = _gelu_tanh(_dg(u_ref[...], h_ref[...], 1, 1))
    parts = []
    for ii in range(groups):
        irow = j * groups + ii
        w = jnp.zeros((PEER_NKEYS, h_ref.shape[0]), BF16)
        for h in range(PEER_HEADS):
            c1 = c1_ref[h, pl.ds(irow, 1), :].astype(BF16)
            e1 = e1z_ref[h, pl.ds(irow, 1), :].astype(BF16)
            w = w + jnp.where(r2_ref[h] < c1, e2_ref[h] * e1, jnp.zeros_like(w))
        parts.append(w.astype(F32) * act[ii * PEER_NKEYS:(ii + 1) * PEER_NKEYS, :])
    x = jnp.concatenate(parts, axis=0)
    acc[...] += _dg(x.T.astype(BF16), v_ref[...])

    @pl.when(j == pl.num_programs(1) - 1)
    def _():
        o_ref[...] = acc[...]


def _peer_dense(h2, u, v, r2, e2, c1, e1z, tm):
    N, D = h2.shape
    NE = u.shape[0]
    te = 512
    tm = min(tm, N)
    tok = lambda: pl.BlockSpec((PEER_HEADS, PEER_NKEYS, tm), lambda i, j: (0, 0, i))
    return pl.pallas_call(
        _peer_dense_body,
        out_shape=jax.ShapeDtypeStruct((N, D), F32),
        grid=(N // tm, NE // te),
        in_specs=[pl.BlockSpec((tm, D), lambda i, j: (i, 0)), pl.BlockSpec((te, D), lambda i, j: (j, 0)),
                  pl.BlockSpec((te, D), lambda i, j: (j, 0)), tok(), tok(), tok(), tok()],
        out_specs=pl.BlockSpec((tm, D), lambda i, j: (i, 0)),
        scratch_shapes=[pltpu.VMEM((tm, D), F32)],
        compiler_params=_cparams(("parallel", "arbitrary"), 56),
        name="peer_dense",
    )(h2, u, v, r2, e2, c1, e1z)


def _peer(h2, wq, keys, u, v, tm):
    qh = _matmul(h2, wq, tn=1024, name="peer_query")
    r2, e2, c1, e1z = _peer_route(qh, keys)
    return _peer_dense(h2, u, v, r2, e2, c1, e1z, tm)


def _in_proj_weights(w_in):
    D = w_in.shape[0]
    nq = NSA_HEADS * HEAD_DIM
    nkv = 6 * NSA_KV * HEAD_DIM
    ng = NSA_HEADS * 3
    ndn = DN_HEADS * (2 * DN_DK + DN_DV)
    nz = DN_HEADS * DN_DV
    o_gate = nq + nkv
    o_dn = o_gate + ng
    o_a = o_dn + ndn
    o_b = o_a + DN_HEADS
    o_z = o_b + DN_HEADS
    o_m = o_z + nz
    small = jnp.concatenate([w_in[:, o_gate:o_dn], w_in[:, o_a:o_z],
                             jnp.zeros((D, 128 - ng - 2 * DN_HEADS), w_in.dtype)], axis=1)
    cast = lambda a: a.astype(BF16)
    return (cast(w_in[:, :o_gate]), cast(small), cast(w_in[:, o_dn:o_a]), cast(w_in[:, o_z:o_m]),
            cast(w_in[:, o_m:]))


def _project(h, wts):
    w_qkv, w_small, w_dn, w_z, w_merge = wts
    return (_matmul(h, w_qkv, tn=512, name="proj_qkv"), _matmul(h, w_small, tn=128, name="proj_small"),
            _matmul(h, w_dn, tn=1024, name="proj_dn"), _matmul(h, w_z, tn=1024, name="proj_z"),
            _matmul(h, w_merge, tn=1024, name="proj_merge"))


def _gate_groups(small, M):
    return small[:, GATE_COL:GATE_COL + 3 * NSA_HEADS].reshape(M, NSA_KV, 3 * NSA_REP).transpose(1, 0, 2)


def kernel(x_prompt, x_sample, cache_cmp_k, cache_cmp_v, cache_slc_k, cache_slc_v, state_win_k, state_win_v,
           state_conv, state_delta, page_table, c_prompt, c_sample, w_ada, b_ada, norm1_w, norm2_w, w_in,
           q_norm_w, k_norm_w, w_cmp_k, b_cmp_k, w_cmp_v, b_cmp_v, dn_conv_w, dn_A_log, dn_dt_bias, dn_norm_w,
           w_br_a, w_br_b, w_out, w_peer_q, w_peer_keys, w_peer_u, w_peer_v):
    depth = w_in.shape[0]
    B, T, D = x_prompt.shape
    DB = x_sample.shape[0]
    assert x_sample.shape[1] == 1
    NP = B * T
    page = cache_cmp_k.shape[2]
    past_len = page_table.shape[1] * page
    gw = NSA_KV * HEAD_DIM

    yp = x_prompt.reshape(NP, D)
    ys = x_sample.reshape(DB, D)
    p_hist, s_hist = [], []
    cos_p, sin_p = _rope_tables(np.arange(T))
    cos_s, sin_s = _rope_tables(np.full((DB,), past_len))
    n_cp = T // CMP_STRIDE
    cos_cp, sin_cp = _rope_tables(np.arange(n_cp) * CMP_STRIDE + CMP_LEN - 1)
    n_cs = past_len // CMP_STRIDE
    cos_cs, sin_cs = _rope_tables(np.arange(n_cs) * CMP_STRIDE + CMP_LEN - 1)
    n_ada = -(-(B + DB) // 16) * 16
    tile2 = lambda a: jnp.concatenate([a] * NSA_KV, axis=-1)

    for l in range(depth):
        c_all = jnp.concatenate([c_prompt, c_sample, jnp.zeros((n_ada - B - DB, D), F32)], axis=0)
        ada = _matmul(c_all, w_ada[l], tn=512, bias=b_ada[l], name="adaln")
        mods = [ada[:, i * D:(i + 1) * D] for i in range(6)]
        mp = [m[:B] for m in mods]
        ms = [m[B:B + DB] for m in mods]

        wts = _in_proj_weights(w_in[l])
        wa, wb, wo = w_br_a[l].astype(BF16), w_br_b[l].astype(BF16), w_out[l].astype(BF16)
        wq = w_peer_q[l].transpose(1, 0, 2).reshape(D, PEER_HEADS * PEER_DKEY).astype(BF16)
        pu, pv = w_peer_u[l].astype(BF16), w_peer_v[l].astype(BF16)
        wck, wcv = tile2(w_cmp_k[l]), tile2(w_cmp_v[l])
        bck, bcv = tile2(b_cmp_k[l]).reshape(1, gw), tile2(b_cmp_v[l]).reshape(1, gw)

        hp = _normmod(yp, norm1_w[l], mp[1], mp[0], T)
        qkv, small, dnx, z, mg = _project(hp, wts)
        q, kc, vc, ks, vs, kw, vw, ksb, vsb, kwb, vwb = _qkprep(qkv, cos_p, sin_p, q_norm_w[l], k_norm_w[l], T)
        ck = _compress_prompt(kc.reshape(B, T, gw), wck, bck, cos_cp, sin_cp, True)
        cv = _compress_prompt(vc.reshape(B, T, gw), wcv, bcv, cos_cp, sin_cp, False)
        o_nsa = _nsa_prompt(q, _gate_groups(small, NP), ck, cv, ksb, vsb, kwb, vwb, B, T)
        o_dn, p_delta = _dn_prompt(dnx, z, small, dn_conv_w[l], dn_A_log[l], dn_dt_bias[l], dn_norm_w[l], B, T)
        mixed = _merge(o_nsa, o_dn, mg, wa, wb)
        x1p, h2p = _outproj(yp, mixed, wo, mp[2], norm2_w[l], mp[4], mp[3], T)

        keep = min(WINDOW, T)
        r5 = lambda a: a.reshape(B, T, NSA_KV, HEAD_DIM)
        p_conv = dnx.reshape(B, T, -1)[:, T - (DN_CONV - 1):]
        p_hist.append((r5(kc), r5(vc), r5(ks), r5(vs), r5(kw)[:, T - keep:], r5(vw)[:, T - keep:], p_conv, p_delta))

        hs = _normmod(ys, norm1_w[l], ms[1], ms[0], 1)
        qkv, small, dnx, z, mg = _project(hs, wts)
        q, kc, vc, ks, vs, kw, vw, _, _, _, _ = _qkprep(qkv, cos_s, sin_s, q_norm_w[l], k_norm_w[l], 1)
        pool = cache_cmp_k.shape[1]
        rows = lambda c: c[l].reshape(c.shape[1], c.shape[2] * NSA_KV, HEAD_DIM)
        q8 = q.reshape(DB, NSA_HEADS, HEAD_DIM)
        ocmp, idx = _nsa_dec1(q8, page_table, rows(cache_cmp_k), rows(cache_cmp_v),
                              _interleaved_cmp_weights(w_cmp_k[l]), bck, _interleaved_cmp_weights(w_cmp_v[l]), bcv,
                              cos_cs, sin_cs, past_len)
        win = state_win_k.shape[2]
        gate4 = small[:, GATE_COL:GATE_COL + 3 * NSA_HEADS].reshape(DB, NSA_KV, NSA_REP, 3)
        r4 = lambda a: a.reshape(DB, NSA_KV, NSA_REP, HEAD_DIM)
        r3 = lambda a: a.reshape(DB, NSA_KV, HEAD_DIM)
        o8, s_win_k, s_win_v = _nsa_dec2(
            r4(q), gate4, r4(ocmp), page_table, idx[..., 0], rows(cache_slc_k), rows(cache_slc_v), r3(ks), r3(vs),
            rows(state_win_k), rows(state_win_v), r3(kw), r3(vw), past_len)
        o_dn, s_conv, s_delta = _dn_decode(dnx, state_conv[l], small, z, state_delta[l], dn_conv_w[l], dn_A_log[l],
                                           dn_dt_bias[l], dn_norm_w[l])
        mixed = _merge(o8.reshape(DB, NSA_HEADS * HEAD_DIM), o_dn.reshape(DB, -1), mg, wa, wb)
        x1s, h2s = _outproj(ys, mixed, wo, ms[2], norm2_w[l], ms[4], ms[3], 1)
        r5s = lambda a: a.reshape(DB, 1, NSA_KV, HEAD_DIM)
        s_hist.append((r5s(kc), r5s(vc), r5s(ks), r5s(vs), s_win_k.reshape(DB, win, NSA_KV, HEAD_DIM),
                       s_win_v.reshape(DB, win, NSA_KV, HEAD_DIM), s_conv, s_delta))

        peer_p = _peer(h2p, wq, w_peer_keys[l], pu, pv, 512)
        n_pad = -(-DB // 128) * 128
        h2s_pad = jnp.concatenate([h2s, jnp.zeros((n_pad - DB, D), h2s.dtype)], axis=0)
        peer_s = _peer(h2s_pad, wq, w_peer_keys[l], pu, pv, n_pad)[:DB]
        yp = _residual(x1p, peer_p, mp[5], T)
        ys = _residual(x1s, peer_s, ms[5], 1)

    stack = lambda hist: [jnp.stack(a) for a in zip(*hist)]
    return (yp.reshape(B, T, D), ys.reshape(DB, 1, D), *stack(p_hist), *stack(s_hist))
```

```python
import functools

import numpy as np
import jax
import jax.numpy as jnp
from jax import lax
from jax.experimental import pallas as pl
from jax.experimental.pallas import tpu as pltpu

F32 = jnp.float32
BF16 = jnp.bfloat16

HEAD_DIM = 128
NSA_HEADS = 8
NSA_KV = 2
NSA_REP = NSA_HEADS // NSA_KV
CMP_STRIDE = 16
CMP_LEN = 2 * CMP_STRIDE
SLC_LEN = 64
SLC_TOPN = 16
WINDOW = 512
DN_HEADS = 8
DN_DK = 128
DN_DV = 128
DN_CONV = 4
DN_CHUNK = 128
PEER_HEADS = 8
PEER_NKEYS = 128
PEER_DKEY = 256
PEER_TOPK = 16
ROPE_THETA = 10000.0
EPS = 1e-6
NEG = -1e30
LOG2E = 1.4426950408889634
PAGE_GROUP = 8
GATE_COL = 0
DNA_COL = 24
DNB_COL = 32
MIB = 2 ** 20


def _cparams(semantics, vmem_mib):
    return pltpu.CompilerParams(dimension_semantics=semantics, vmem_limit_bytes=vmem_mib * MIB)


def _dg(a, b, ca=1, cb=0):
    return lax.dot_general(a, b, (((ca,), (cb,)), ((), ())), preferred_element_type=F32)


def _dot1(a, b, ca=1, cb=0):
    return _dg(a.astype(BF16), b.astype(BF16), ca, cb)


def _split2(a):
    hi = a.astype(BF16)
    return hi, (a - hi.astype(F32)).astype(BF16)


def _split3(a):
    hi = a.astype(BF16)
    r = a - hi.astype(F32)
    mid = r.astype(BF16)
    return hi, mid, (r - mid.astype(F32)).astype(BF16)


def _dot3(a, b, ca=1, cb=0):
    ah, al = _split2(a)
    bh, bl = _split2(b)
    return _dg(ah, bh, ca, cb) + _dg(ah, bl, ca, cb) + _dg(al, bh, ca, cb)


def _dot_sel_rhs(a, sel):
    return sum(_dg(p, sel) for p in _split3(a))


def _dot_sel_lhs(sel, b):
    return sum(_dg(sel, p) for p in _split3(b))


def _sigmoid(x):
    return 1.0 / (1.0 + jnp.exp(-x))


def _silu(x):
    return x * _sigmoid(x)


def _softplus(x):
    return jnp.maximum(x, 0.0) + jnp.log(1.0 + jnp.exp(-jnp.abs(x)))


def _gelu_tanh(x):
    return 0.5 * x * (1.0 + jnp.tanh(0.7978845608028654 * (x + 0.044715 * (x * x * x))))


def _rms(x, w):
    return x * lax.rsqrt(jnp.mean(x * x, axis=-1, keepdims=True) + EPS) * w


def _l2(x):
    return x * lax.rsqrt(jnp.sum(x * x, axis=-1, keepdims=True) + EPS)


def _rope(x, cos, sin_signed):
    return x * cos + pltpu.roll(x, HEAD_DIM // 2, 1) * sin_signed


def _masked_softmax_rows(s, mask):
    s = jnp.where(mask, s, NEG)
    m = jnp.max(s, axis=-1, keepdims=True)
    e = jnp.where(mask, jnp.exp(s - m), 0.0)
    den = jnp.sum(e, axis=-1, keepdims=True)
    return e / jnp.where(den > 0.0, den, 1.0)


def _rank_desc(score, lane_idx, n):
    rank = jnp.zeros(score.shape, F32)
    for s2 in range(n):
        col = score[:, s2:s2 + 1]
        beats = (col > score) | ((col == score) & (lane_idx > s2))
        rank = rank + jnp.where(beats, 1.0, 0.0)
    return rank


def _mm_body(*refs, has_bias):
    if has_bias:
        a_ref, b_ref, bias_ref, o_ref = refs
    else:
        a_ref, b_ref, o_ref = refs
    acc = _dot1(a_ref[...], b_ref[...])
    if has_bias:
        acc = acc + bias_ref[...]
    o_ref[...] = acc.astype(o_ref.dtype)


def _matmul(a, b, *, tn, bias=None, out_dtype=F32, tm_max=1024, name="matmul"):
    M, K = a.shape
    N = b.shape[1]
    tm = min(M, tm_max)
    tn = min(tn, N)
    assert M % tm == 0 and N % tn == 0
    in_specs = [pl.BlockSpec((tm, K), lambda i, j: (i, 0)), pl.BlockSpec((K, tn), lambda i, j: (0, j))]
    args = [a, b]
    if bias is not None:
        in_specs.append(pl.BlockSpec((1, tn), lambda i, j: (0, j)))
        args.append(bias.reshape(1, N))
    return pl.pallas_call(
        functools.partial(_mm_body, has_bias=bias is not None),
        out_shape=jax.ShapeDtypeStruct((M, N), out_dtype),
        grid=(M // tm, N // tn),
        in_specs=in_specs,
        out_specs=pl.BlockSpec((tm, tn), lambda i, j: (i, j)),
        compiler_params=_cparams(("parallel", "arbitrary"), 48),
        name=name,
    )(*args)


def _mod_specs(mod, rows_per_seq, tm, D):
    if rows_per_seq == 1:
        return mod, pl.BlockSpec((tm, D), lambda i: (i, 0))
    assert rows_per_seq % tm == 0
    tps = rows_per_seq // tm
    return mod.reshape(mod.shape[0], 1, D), pl.BlockSpec((None, 1, D), lambda i: (i // tps, 0, 0))


def _normmod_body(x_ref, w_ref, sc_ref, sh_ref, o_ref):
    h = _rms(x_ref[...], w_ref[...])
    o_ref[...] = (h * (1.0 + sc_ref[...]) + sh_ref[...]).astype(o_ref.dtype)


def _normmod(x, w, sc, sh, rows_per_seq):
    M, D = x.shape
    tm = min(512, rows_per_seq) if rows_per_seq > 1 else M
    sc_a, spec = _mod_specs(sc, rows_per_seq, tm, D)
    sh_a, _ = _mod_specs(sh, rows_per_seq, tm, D)
    return pl.pallas_call(
        _normmod_body,
        out_shape=jax.ShapeDtypeStruct((M, D), BF16),
        grid=(M // tm,),
        in_specs=[pl.BlockSpec((tm, D), lambda i: (i, 0)), pl.BlockSpec((1, D), lambda i: (0, 0)), spec, spec],
        out_specs=pl.BlockSpec((tm, D), lambda i: (i, 0)),
        compiler_params=_cparams(("parallel",), 40),
        name="normmod",
    )(x, w.reshape(1, D), sc_a, sh_a)


def _merge_body(on_ref, od_ref, g1_ref, g2_ref, wa_ref, wb_ref, o_ref):
    a = _dot1(on_ref[...], wa_ref[...])
    b = _dot1(od_ref[...], wb_ref[...])
    o_ref[...] = (_sigmoid(g1_ref[...]) * a + _sigmoid(g2_ref[...]) * b).astype(o_ref.dtype)


def _merge(o_nsa, o_dn, mg, wa, wb):
    M, K = o_nsa.shape
    D = wa.shape[1]
    tm = min(M, 512)
    tn = 1024
    nj = D // tn
    return pl.pallas_call(
        _merge_body,
        out_shape=jax.ShapeDtypeStruct((M, D), BF16),
        grid=(M // tm, nj),
        in_specs=[pl.BlockSpec((tm, K), lambda i, j: (i, 0)), pl.BlockSpec((tm, K), lambda i, j: (i, 0)),
                  pl.BlockSpec((tm, tn), lambda i, j: (i, j)), pl.BlockSpec((tm, tn), lambda i, j: (i, j + nj)),
                  pl.BlockSpec((K, tn), lambda i, j: (0, j)), pl.BlockSpec((K, tn), lambda i, j: (0, j))],
        out_specs=pl.BlockSpec((tm, tn), lambda i, j: (i, j)),
        compiler_params=_cparams(("parallel", "arbitrary"), 40),
        name="merge",
    )(o_nsa, o_dn, mg, mg, wa, wb)


def _outproj_body(x_ref, mx_ref, w_ref, gt_ref, nw_ref, sc_ref, sh_ref, x1_o, h2_o):
    x1 = x_ref[...] + gt_ref[...] * _dot1(mx_ref[...], w_ref[...])
    x1_o[...] = x1
    h2_o[...] = (_rms(x1, nw_ref[...]) * (1.0 + sc_ref[...]) + sh_ref[...]).astype(h2_o.dtype)


def _outproj(x, mixed, w_out, gt1, norm2_w, sc2, sh2, rows_per_seq):
    M, D = x.shape
    tm = min(256, rows_per_seq) if rows_per_seq > 1 else M
    gt_a, spec = _mod_specs(gt1, rows_per_seq, tm, D)
    sc_a, _ = _mod_specs(sc2, rows_per_seq, tm, D)
    sh_a, _ = _mod_specs(sh2, rows_per_seq, tm, D)
    row = pl.BlockSpec((tm, D), lambda i: (i, 0))
    return pl.pallas_call(
        _outproj_body,
        out_shape=(jax.ShapeDtypeStruct((M, D), F32), jax.ShapeDtypeStruct((M, D), BF16)),
        grid=(M // tm,),
        in_specs=[row, row, pl.BlockSpec((D, D), lambda i: (0, 0)), spec,
                  pl.BlockSpec((1, D), lambda i: (0, 0)), spec, spec],
        out_specs=(row, row),
        compiler_params=_cparams(("parallel",), 48),
        name="outproj",
    )(x, mixed, w_out, gt_a, norm2_w.reshape(1, D), sc_a, sh_a)


def _residual_body(x_ref, p_ref, gt_ref, o_ref):
    o_ref[...] = x_ref[...] + gt_ref[...] * p_ref[...]


def _residual(x1, peer, gt2, rows_per_seq):
    M, D = x1.shape
    tm = min(512, rows_per_seq) if rows_per_seq > 1 else M
    gt_a, spec = _mod_specs(gt2, rows_per_seq, tm, D)
    row = pl.BlockSpec((tm, D), lambda i: (i, 0))
    return pl.pallas_call(
        _residual_body,
        out_shape=jax.ShapeDtypeStruct((M, D), F32),
        grid=(M // tm,),
        in_specs=[row, row, spec],
        out_specs=row,
        compiler_params=_cparams(("parallel",), 40),
        name="residual",
    )(x1, peer, gt_a)


def _qkprep_body(qkv_ref, cos_ref, sin_ref, qw_ref, kw_ref,
                 q_o, kc_o, vc_o, ks_o, vs_o, kwn_o, vw_o, ksb_o, vsb_o, kwb_o, vwb_o):
    cos = cos_ref[...]
    sin = sin_ref[...]
    for h in range(NSA_HEADS):
        sl = slice(h * HEAD_DIM, (h + 1) * HEAD_DIM)
        q_o[:, sl] = _rope(_rms(qkv_ref[:, sl], qw_ref[...]), cos, sin)
    base = NSA_HEADS * HEAD_DIM
    gw = NSA_KV * HEAD_DIM
    for g in range(NSA_KV):
        sl = slice(g * HEAD_DIM, (g + 1) * HEAD_DIM)

        def col(part):
            return qkv_ref[:, base + part * gw + g * HEAD_DIM: base + part * gw + (g + 1) * HEAD_DIM]

        kc_o[:, sl] = _rms(col(0), kw_ref[0:1, :])
        vc_o[:, sl] = col(1)
        ks = _rope(_rms(col(2), kw_ref[1:2, :]), cos, sin)
        ks_o[:, sl] = ks
        ksb_o[:, sl] = ks.astype(BF16)
        vs = col(3)
        vs_o[:, sl] = vs
        vsb_o[:, sl] = vs.astype(BF16)
        kw = _rope(_rms(col(4), kw_ref[2:3, :]), cos, sin)
        kwn_o[:, sl] = kw
        kwb_o[:, sl] = kw.astype(BF16)
        vw = col(5)
        vw_o[:, sl] = vw
        vwb_o[:, sl] = vw.astype(BF16)


def _qkprep(qkv, cos, sin, q_norm_w, k_norm_w, rows_per_seq):
    M, W = qkv.shape
    tm = min(256, rows_per_seq) if rows_per_seq > 1 else M
    tps = max(rows_per_seq // tm, 1)
    gw = NSA_KV * HEAD_DIM
    row = lambda w: pl.BlockSpec((tm, w), lambda i: (i, 0))
    tab = pl.BlockSpec((tm, HEAD_DIM), lambda i: (i % tps, 0))
    shapes = [jax.ShapeDtypeStruct((M, NSA_HEADS * HEAD_DIM), F32)] + [jax.ShapeDtypeStruct((M, gw), F32)] * 6 \
        + [jax.ShapeDtypeStruct((M, gw), BF16)] * 4
    return pl.pallas_call(
        _qkprep_body,
        out_shape=tuple(shapes),
        grid=(M // tm,),
        in_specs=[row(W), tab, tab, pl.BlockSpec((1, HEAD_DIM), lambda i: (0, 0)),
                  pl.BlockSpec((3, HEAD_DIM), lambda i: (0, 0))],
        out_specs=tuple([row(NSA_HEADS * HEAD_DIM)] + [row(gw)] * 10),
        compiler_params=_cparams(("parallel",), 40),
        name="qkprep",
    )(qkv, cos, sin, q_norm_w.reshape(1, HEAD_DIM), k_norm_w)


def _rope_tables(pos):
    half = HEAD_DIM // 2
    inv = ROPE_THETA ** (-jnp.arange(half, dtype=F32) / half)
    ang = jnp.asarray(pos, F32)[:, None] * inv
    cos, sin = jnp.cos(ang), jnp.sin(ang)
    return jnp.concatenate([cos, cos], axis=-1), jnp.concatenate([-sin, sin], axis=-1)


def _compress_chunks(x, w_ref, n):
    xr = x.reshape(n, CMP_STRIDE, x.shape[-1])
    first = jnp.sum(xr * w_ref[0:CMP_STRIDE, :][None], axis=1)
    second = jnp.sum(xr * w_ref[CMP_STRIDE:CMP_LEN, :][None], axis=1)
    return first, second


def _finish_compress(first, second, b_ref, cos_ref, sin_ref, rope):
    n = first.shape[0]
    c = first + pltpu.roll(second, n - 1, 0) + b_ref[...]
    if not rope:
        return c
    parts = [_rope(c[:, g * HEAD_DIM:(g + 1) * HEAD_DIM], cos_ref[...], sin_ref[...]) for g in range(NSA_KV)]
    return jnp.concatenate(parts, axis=-1)


def _compress_body(x_ref, w_ref, b_ref, cos_ref, sin_ref, o_ref, *, rope):
    n = x_ref.shape[0] // CMP_STRIDE
    first, second = _compress_chunks(x_ref[...], w_ref, n)
    o_ref[...] = _finish_compress(first, second, b_ref, cos_ref, sin_ref, rope)


def _compress_prompt(x, w, b, cos_c, sin_c, rope):
    B, T, W = x.shape
    n = T // CMP_STRIDE
    full = lambda s: pl.BlockSpec(s, lambda i: (0,) * len(s))
    return pl.pallas_call(
        functools.partial(_compress_body, rope=rope),
        out_shape=jax.ShapeDtypeStruct((B, n, W), F32),
        grid=(B,),
        in_specs=[pl.BlockSpec((None, T, W), lambda i: (i, 0, 0)), full((CMP_LEN, W)), full((1, W)),
                  full((n, HEAD_DIM)), full((n, HEAD_DIM))],
        out_specs=pl.BlockSpec((None, n, W), lambda i: (i, 0, 0)),
        compiler_params=_cparams(("parallel",), 40),
        name="compress",
    )(x, w, b, cos_c, sin_c)


def _overlap_matrix(n_rows, n_cmp, n_cols, n_slc):
    cs = np.arange(n_rows)[:, None] * CMP_STRIDE
    ss = np.arange(n_cols)[None, :] * SLC_LEN
    m = (cs < ss + SLC_LEN) & (cs + CMP_LEN > ss)
    m &= (np.arange(n_rows)[:, None] < n_cmp) & (np.arange(n_cols)[None, :] < n_slc)
    return jnp.asarray(m.astype(np.float32), BF16)


def _nsa_prompt_body(q_ref, gate_ref, ck_ref, cv_ref, ks_ref, vs_ref, kw_ref, vw_ref, ov_ref, ex_ref, o_ref,
                     *, tq, T, n_cmp, n_slc, n_sel, span):
    t0 = pl.program_id(2) * tq
    scale = HEAD_DIM ** -0.5
    qpos = t0 + lax.broadcasted_iota(jnp.int32, (tq, 1), 0)
    qs = [q_ref[:, r * HEAD_DIM:(r + 1) * HEAD_DIM] for r in range(NSA_REP)]

    ncp = ck_ref.shape[0]
    nidx = lax.broadcasted_iota(jnp.int32, (1, ncp), 1)
    cmask = ((nidx * CMP_STRIDE + CMP_LEN - 1) <= qpos) & (nidx < n_cmp)
    ck = ck_ref[...]
    cv = cv_ref[...].astype(BF16)
    p_sum = jnp.zeros((tq, ncp), F32)
    o_cmp = []
    for r in range(NSA_REP):
        p = _masked_softmax_rows(_dot3(qs[r], ck, 1, 1) * scale, cmask)
        p_sum = p_sum + p
        o_cmp.append(_dot1(p, cv))

    imp = _dot_sel_rhs(p_sum, ov_ref[...])
    blk = lax.broadcasted_iota(jnp.int32, (1, n_slc), 1)
    cur = qpos // SLC_LEN
    visible = blk <= cur
    forced = (blk == 0) | (blk == cur) | (blk == cur - 1)
    score = jnp.where(forced, 1e9, jnp.where(visible, imp, -1e9))
    rank = _rank_desc(score, blk, n_slc)
    sel = jnp.where((rank < n_sel) & visible, 1.0, 0.0).astype(BF16)

    qb = [(qs[r] * (scale * LOG2E)).astype(BF16) for r in range(NSA_REP)]

    kc = ex_ref.shape[2]

    def chunk(c, carry):
        k0 = pl.multiple_of(c * kc, kc)
        kk = ks_ref[pl.ds(k0, kc), :]
        vv = vs_ref[pl.ds(k0, kc), :]
        kpos = k0 + lax.broadcasted_iota(jnp.int32, (1, kc), 1)
        bias = jnp.where((_dg(sel, ex_ref[c]) > 0.5) & (kpos <= qpos), 0.0, NEG)
        out = []
        for r in range(NSA_REP):
            m, l, acc = carry[r]
            s = _dg(qb[r], kk, 1, 1) + bias
            m_new = jnp.maximum(m, jnp.max(s, axis=-1, keepdims=True))
            alpha = jnp.exp2(m - m_new)
            e = jnp.exp2(s - m_new)
            out.append((m_new, alpha * l + jnp.sum(e, axis=-1, keepdims=True), alpha * acc + _dot1(e, vv)))
        return tuple(out)

    init = tuple((jnp.full((tq, 1), NEG, F32), jnp.zeros((tq, 1), F32), jnp.zeros((tq, HEAD_DIM), F32))
                 for _ in range(NSA_REP))
    o_slc = [acc / l for _, l, acc in lax.fori_loop(0, (t0 + tq + kc - 1) // kc, chunk, init)]

    kstart = pl.multiple_of(jnp.clip(t0 - WINDOW, 0, T - span), tq)
    kw = kw_ref[pl.ds(kstart, span), :]
    vw = vw_ref[pl.ds(kstart, span), :]
    kposw = kstart + lax.broadcasted_iota(jnp.int32, (1, span), 1)
    bias_w = jnp.where((kposw <= qpos) & (kposw >= qpos - WINDOW), 0.0, NEG)
    gt = _sigmoid(gate_ref[...])
    for r in range(NSA_REP):
        s = _dg(qb[r], kw, 1, 1) + bias_w
        e = jnp.exp2(s - jnp.max(s, axis=-1, keepdims=True))
        o_win = _dot1(e, vw) / jnp.sum(e, axis=-1, keepdims=True)
        o = (gt[:, 3 * r:3 * r + 1] * o_cmp[r] + gt[:, 3 * r + 1:3 * r + 2] * o_slc[r]
             + gt[:, 3 * r + 2:3 * r + 3] * o_win)
        o_ref[:, r * HEAD_DIM:(r + 1) * HEAD_DIM] = o.astype(o_ref.dtype)


def _nsa_prompt(q, gate_g, ck, cv, ksb, vsb, kwb, vwb, B, T):
    tq = 128
    nq = T // tq
    n_cmp = (T - CMP_LEN) // CMP_STRIDE + 1
    ncp = T // CMP_STRIDE
    n_slc = -(-T // SLC_LEN)
    n_sel = min(SLC_TOPN, n_slc)
    span = min(WINDOW + tq, T)
    gw = NSA_REP * HEAD_DIM
    ov = _overlap_matrix(ncp, n_cmp, n_slc, n_slc)
    kc = min(256, T)
    assert T % kc == 0 and kc % SLC_LEN == 0
    ex = (np.arange(T)[None, :] // SLC_LEN == np.arange(n_slc)[:, None]).astype(np.float32)
    ex = jnp.asarray(ex.reshape(n_slc, T // kc, kc).transpose(1, 0, 2), BF16)
    seq = lambda: pl.BlockSpec((None, T, HEAD_DIM), lambda b, g, i: (b, 0, g))
    cmp_spec = lambda: pl.BlockSpec((None, ncp, HEAD_DIM), lambda b, g, i: (b, 0, g))
    body = functools.partial(_nsa_prompt_body, tq=tq, T=T, n_cmp=n_cmp, n_slc=n_slc, n_sel=n_sel, span=span)
    return pl.pallas_call(
        body,
        out_shape=jax.ShapeDtypeStruct((B * T, NSA_HEADS * HEAD_DIM), BF16),
        grid=(B, NSA_KV, nq),
        in_specs=[pl.BlockSpec((tq, gw), lambda b, g, i: (b * nq + i, g)),
                  pl.BlockSpec((None, tq, 3 * NSA_REP), lambda b, g, i: (g, b * nq + i, 0)),
                  cmp_spec(), cmp_spec(), seq(), seq(), seq(), seq(),
                  pl.BlockSpec((ncp, n_slc), lambda b, g, i: (0, 0)),
                  pl.BlockSpec(ex.shape, lambda b, g, i: (0, 0, 0))],
        out_specs=pl.BlockSpec((tq, gw), lambda b, g, i: (b * nq + i, g)),
        compiler_params=_cparams(("parallel", "parallel", "arbitrary"), 48),
        name="nsa_prompt",
    )(q, gate_g, ck, cv, ksb.reshape(B, T, -1), vsb.reshape(B, T, -1), kwb.reshape(B, T, -1),
      vwb.reshape(B, T, -1), ov, ex)


def _interleaved_cmp_weights(w):
    out = []
    for g in range(NSA_KV):
        for half in range(2):
            wh = w[half * CMP_STRIDE:(half + 1) * CMP_STRIDE]
            z = jnp.zeros_like(wh)
            parts = [wh if gg == g else z for gg in range(NSA_KV)]
            out.append(jnp.stack(parts, axis=1).reshape(CMP_STRIDE * NSA_KV, w.shape[1]))
    return jnp.stack(out)


def _nsa_dec1_body(pt_ref, q_ref, *refs, n_cmp, n_slc, qpos):
    PG = PAGE_GROUP
    kp, vp = refs[:PG], refs[PG:2 * PG]
    wk, bk, wv, bv, cos_ref, sin_ref, ov_ref = refs[2 * PG:2 * PG + 7]
    ocmp_o, idx_o = refs[2 * PG + 7:2 * PG + 9]
    fk, sk, fv, sv = refs[2 * PG + 9:]
    pg = pl.program_id(1)
    chunk = CMP_STRIDE * NSA_KV
    rows_pp = kp[0].shape[0] // chunk
    for p in range(PG):
        base = pl.multiple_of((pg * PG + p) * rows_pp, rows_pp)
        xk = kp[p][...].reshape(rows_pp, chunk, HEAD_DIM)
        xv = vp[p][...].reshape(rows_pp, chunk, HEAD_DIM)
        for g in range(NSA_KV):
            sl = slice(g * HEAD_DIM, (g + 1) * HEAD_DIM)
            fk[pl.ds(base, rows_pp), sl] = jnp.sum(xk * wk[2 * g][None], axis=1)
            sk[pl.ds(base, rows_pp), sl] = jnp.sum(xk * wk[2 * g + 1][None], axis=1)
            fv[pl.ds(base, rows_pp), sl] = jnp.sum(xv * wv[2 * g][None], axis=1)
            sv[pl.ds(base, rows_pp), sl] = jnp.sum(xv * wv[2 * g + 1][None], axis=1)

    @pl.when(pg == pl.num_programs(1) - 1)
    def _():
        scale = HEAD_DIM ** -0.5
        ncp = fk.shape[0]
        ck = _finish_compress(fk[...], sk[...], bk, cos_ref, sin_ref, True)
        cv = _finish_compress(fv[...], sv[...], bv, cos_ref, sin_ref, False)
        q8 = q_ref[...]
        nidx = lax.broadcasted_iota(jnp.int32, (1, ncp), 1)
        cmask = ((nidx * CMP_STRIDE + CMP_LEN - 1) <= qpos) & (nidx < n_cmp)
        rowi = lax.broadcasted_iota(jnp.int32, (NSA_HEADS, 1), 0)
        ocmp = jnp.zeros((NSA_HEADS, HEAD_DIM), F32)
        p2 = jnp.zeros((NSA_HEADS, ncp), F32)
        for g in range(NSA_KV):
            sl = slice(g * HEAD_DIM, (g + 1) * HEAD_DIM)
            p = _masked_softmax_rows(_dot3(q8, ck[:, sl], 1, 1) * scale, cmask)
            in_g = (rowi >= g * NSA_REP) & (rowi < (g + 1) * NSA_REP)
            ocmp = jnp.where(in_g, _dot1(p, cv[:, sl]), ocmp)
            ps = jnp.sum(jnp.where(in_g, p, 0.0), axis=0, keepdims=True)
            p2 = jnp.where(rowi == g, ps, p2)
        ocmp_o[...] = ocmp

        imp = _dot_sel_rhs(p2, ov_ref[...])
        nsp = ov_ref.shape[1]
        blk = lax.broadcasted_iota(jnp.int32, (1, nsp), 1)
        cur = qpos // SLC_LEN
        forced = (blk == 0) | (blk == cur) | (blk == cur - 1)
        score = jnp.where(forced, 1e9, jnp.where(blk <= cur, imp, -1e9))
        score = jnp.where(blk < n_slc, score, -2e9)
        rank = _rank_desc(score, blk, n_slc)
        kcol = lax.broadcasted_iota(jnp.int32, (SLC_TOPN, 1), 0).astype(F32)
        blkf = blk.astype(F32)
        for g in range(NSA_KV):
            hit = rank[g:g + 1, :] == kcol
            ids = jnp.sum(jnp.where(hit, blkf, 0.0), axis=1, keepdims=True)
            idx_o[g] = jnp.broadcast_to(ids, (SLC_TOPN, HEAD_DIM)).astype(jnp.int32)


def _nsa_dec1(q8, page_table, cache_k, cache_v, wk, bk, wv, bv, cos_c, sin_c, past_len):
    DB, n_pages = page_table.shape
    PG = PAGE_GROUP
    assert n_pages % PG == 0
    prow = cache_k.shape[1]
    W = NSA_KV * HEAD_DIM
    ncp = past_len // CMP_STRIDE
    n_cmp = (past_len + 1 - CMP_LEN) // CMP_STRIDE + 1
    n_slc = -(-(past_len + 1) // SLC_LEN)
    nsp = -(-n_slc // 128) * 128
    assert past_len // SLC_LEN + 1 == n_slc and n_slc >= SLC_TOPN
    ov = _overlap_matrix(ncp, n_cmp, nsp, n_slc)

    def page_spec(p):
        return pl.BlockSpec((None, prow, HEAD_DIM), lambda b, pg, pt: (pt[b, pg * PG + p], 0, 0))

    full = lambda s: pl.BlockSpec(s, lambda b, pg, pt: (0,) * len(s))
    in_specs = [pl.BlockSpec((None, NSA_HEADS, HEAD_DIM), lambda b, pg, pt: (b, 0, 0))]
    in_specs += [page_spec(p) for p in range(PG)] + [page_spec(p) for p in range(PG)]
    in_specs += [full(wk.shape), full((1, W)), full(wv.shape), full((1, W)),
                 full((ncp, HEAD_DIM)), full((ncp, HEAD_DIM)), full((ncp, nsp))]
    grid_spec = pltpu.PrefetchScalarGridSpec(
        num_scalar_prefetch=1, grid=(DB, n_pages // PG), in_specs=in_specs,
        out_specs=(pl.BlockSpec((None, NSA_HEADS, HEAD_DIM), lambda b, pg, pt: (b, 0, 0)),
                   pl.BlockSpec((None, NSA_KV, SLC_TOPN, HEAD_DIM), lambda b, pg, pt: (b, 0, 0, 0))),
        scratch_shapes=[pltpu.VMEM((ncp, W), F32)] * 4)
    body = functools.partial(_nsa_dec1_body, n_cmp=n_cmp, n_slc=n_slc, qpos=past_len)
    return pl.pallas_call(
        body,
        out_shape=(jax.ShapeDtypeStruct((DB, NSA_HEADS, HEAD_DIM), F32),
                   jax.ShapeDtypeStruct((DB, NSA_KV, SLC_TOPN, HEAD_DIM), jnp.int32)),
        grid_spec=grid_spec,
        compiler_params=_cparams(("parallel", "arbitrary"), 40),
        name="nsa_dec_cmp",
    )(page_table, q8, *([cache_k] * PG), *([cache_v] * PG), wk, bk, wv, bv, cos_c, sin_c, ov)


def _attend_rows(q, g, keys, k_new, vals, v_new, valid, scale):
    qb = q.astype(BF16)
    s_new = jnp.sum(q * k_new, axis=-1, keepdims=True) * scale
    s_blk = []
    m = s_new
    for kk, ok in zip(keys, valid):
        s = _dg(qb, kk[...].astype(BF16), 1, 1) * scale
        own = (lax.broadcasted_iota(jnp.int32, (1, s.shape[1]), 1) & (NSA_KV - 1)) == g
        s = jnp.where(own if ok is None else own & ok, s, NEG)
        s_blk.append(s)
        m = jnp.maximum(m, jnp.max(s, axis=-1, keepdims=True))
    e_new = jnp.exp(s_new - m)
    den = e_new
    acc = e_new * v_new
    for s, vv in zip(s_blk, vals):
        e = jnp.exp(s - m)
        den = den + jnp.sum(e, axis=-1, keepdims=True)
        acc = acc + _dot1(e, vv[...])
    return acc / den


def _shift_in(buf, new_rows):
    n, G = buf.shape[0], new_rows.shape[0]
    out = pltpu.roll(buf, n - G, 0)
    rowi = lax.broadcasted_iota(jnp.int32, (n, 1), 0)
    for g in range(G):
        out = jnp.where(rowi == n - G + g, new_rows[g:g + 1, :], out)
    return out


def _nsa_dec2_body(pt_ref, idx_ref, q_ref, gate_ref, ocmp_ref, *refs, n_slc):
    n = SLC_TOPN
    kb, vb = refs[:n], refs[n:2 * n]
    kn, vn, wk, wv, wkn, wvn = refs[2 * n:2 * n + 6]
    o_ref, wko, wvo = refs[2 * n + 6:]
    b = pl.program_id(0)
    g = pl.program_id(1)
    scale = HEAD_DIM ** -0.5
    q = q_ref[...]
    valid = [idx_ref[(b * NSA_KV + g) * n + k] < n_slc - 1 for k in range(n)]
    own = pl.ds(g, 1)
    o_slc = _attend_rows(q, g, kb, kn[own, :], vb, vn[own, :], valid, scale)
    o_win = _attend_rows(q, g, [wk], wkn[own, :], [wv], wvn[own, :], [None], scale)
    gt = _sigmoid(gate_ref[...])
    o_ref[...] = gt[:, 0:1] * ocmp_ref[...] + gt[:, 1:2] * o_slc + gt[:, 2:3] * o_win

    @pl.when(g == 0)
    def _():
        wko[...] = _shift_in(wk[...], wkn[...])
        wvo[...] = _shift_in(wv[...], wvn[...])


def _nsa_dec2(q4, gate4, ocmp4, page_table, idx, slc_k, slc_v, k_new, v_new, win_k, win_v, wk_new, wv_new,
              past_len):
    DB = q4.shape[0]
    n = SLC_TOPN
    n_slc = -(-(past_len + 1) // SLC_LEN)
    brow = SLC_LEN * NSA_KV
    per_page = slc_k.shape[1] // brow
    hk = slc_k.reshape(slc_k.shape[0] * per_page, brow, HEAD_DIM)
    hv = slc_v.reshape(slc_v.shape[0] * per_page, brow, HEAD_DIM)
    wrow = win_k.shape[1]

    def blk_spec(k):
        def imap(b, g, pt, ix):
            s = jnp.minimum(ix[(b * NSA_KV + g) * n + k], n_slc - 2)
            return (pt[b, s // per_page] * per_page + s % per_page, 0, 0)
        return pl.BlockSpec((None, brow, HEAD_DIM), imap)

    head = lambda w: pl.BlockSpec((None, None, NSA_REP, w), lambda b, g, pt, ix: (b, g, 0, 0))
    new = lambda: pl.BlockSpec((None, NSA_KV, HEAD_DIM), lambda b, g, pt, ix: (b, 0, 0))
    wspec = lambda: pl.BlockSpec((None, wrow, HEAD_DIM), lambda b, g, pt, ix: (b, 0, 0))
    blocks = lambda: [blk_spec(k) for k in range(n)]
    in_specs = [head(HEAD_DIM), head(3), head(HEAD_DIM)] + blocks() + blocks() \
        + [new(), new(), wspec(), wspec(), new(), new()]
    grid_spec = pltpu.PrefetchScalarGridSpec(
        num_scalar_prefetch=2, grid=(DB, NSA_KV), in_specs=in_specs,
        out_specs=(head(HEAD_DIM), wspec(), wspec()))
    return pl.pallas_call(
        functools.partial(_nsa_dec2_body, n_slc=n_slc),
        out_shape=(jax.ShapeDtypeStruct(q4.shape, F32),
                   jax.ShapeDtypeStruct(win_k.shape, F32), jax.ShapeDtypeStruct(win_v.shape, F32)),
        grid_spec=grid_spec,
        compiler_params=_cparams(("parallel", "arbitrary"), 40),
        name="nsa_dec_attend",
    )(page_table, idx.reshape(-1), q4, gate4, ocmp4, *([hk] * n), *([hv] * n),
      k_new, v_new, win_k, win_v, wk_new, wv_new)


def _dn_gates(sm, al_ref, dtb_ref):
    g = -jnp.exp(al_ref[...]) * _softplus(sm + dtb_ref[...])
    return g, _sigmoid(sm)


def _dn_out(o, z, nw_ref):
    return _rms(o, nw_ref[...]) * _silu(z)


def _dn_prompt_body(x_ref, z_ref, sm_ref, aT_ref, cw_ref, al_ref, alT_ref, dtb_ref, dtbT_ref, nw_ref,
                    o_ref, s_out_ref, xbuf, S):
    C = DN_CHUNK
    c = pl.program_id(1)

    @pl.when(c == 0)
    def _():
        xbuf[0:8, :] = jnp.zeros((8, xbuf.shape[1]), F32)
        S[...] = jnp.zeros(S.shape, F32)

    xbuf[8:8 + C, :] = x_ref[...]
    lo = 8 - (DN_CONV - 1)
    y = cw_ref[0:1, :] * xbuf[lo:lo + C, :]
    for j in range(1, DN_CONV):
        y = y + cw_ref[j:j + 1, :] * xbuf[lo + j:lo + j + C, :]
    xbuf[0:8, :] = xbuf[C:C + 8, :]
    y = _silu(y)

    g, beta = _dn_gates(sm_ref[...], al_ref, dtb_ref)
    gT = -jnp.exp(alT_ref[...]) * _softplus(aT_ref[...] + dtbT_ref[...])
    row = lax.broadcasted_iota(jnp.int32, (C, C), 0)
    col = lax.broadcasted_iota(jnp.int32, (C, C), 1)
    tril = row >= col
    strict = row > col
    G = _dot_sel_lhs(jnp.where(tril, 1.0, 0.0).astype(BF16), g)
    GT = _dot_sel_rhs(gT, jnp.where(row <= col, 1.0, 0.0).astype(BF16))
    eye = jnp.where(row == col, 1.0, 0.0)
    pair = jnp.right_shift(row, 1) == jnp.right_shift(col, 1)
    merges = []
    for lvl in range(1, int(np.log2(C))):
        merges.append((jnp.right_shift(row, lvl + 1) == jnp.right_shift(col, lvl + 1))
                      & (jnp.right_shift(row, lvl) != jnp.right_shift(col, lvl)))
    nh = DN_HEADS
    hs = range(nh)
    q = [_l2(y[:, h * DN_DK:(h + 1) * DN_DK]) for h in hs]
    k = [_l2(y[:, (nh + h) * DN_DK:(nh + h + 1) * DN_DK]) for h in hs]
    v = [y[:, 2 * nh * DN_DK + h * DN_DV:2 * nh * DN_DK + (h + 1) * DN_DV] for h in hs]
    Gc = [G[:, DNA_COL + h:DNA_COL + h + 1] for h in hs]
    bc = [beta[:, DNB_COL + h:DNB_COL + h + 1] for h in hs]
    decay = [jnp.where(tril, jnp.exp(jnp.where(tril, Gc[h] - GT[h:h + 1, :], 0.0)), 0.0) for h in hs]
    qc = [q[h] * (DN_DK ** -0.5) for h in hs]
    kb = [k[h] * bc[h] for h in hs]
    M = [jnp.where(strict, _dot1(kb[h], k[h], 1, 1) * decay[h], 0.0) for h in hs]
    P = [eye - jnp.where(pair, M[h], 0.0) for h in hs]
    for off in merges:
        W = [_dot3(jnp.where(off, M[h], 0.0), P[h]) for h in hs]
        P = [P[h] - _dot3(P[h], W[h]) for h in hs]
    eG = [jnp.exp(Gc[h]) for h in hs]
    uw = [_dot3(P[h], jnp.concatenate([v[h] * bc[h], kb[h] * eG[h]], axis=1)) for h in hs]
    aqk = [jnp.where(tril, _dot1(qc[h], k[h], 1, 1) * decay[h], 0.0) for h in hs]
    Sh = [S[h] for h in hs]
    v_new = [uw[h][:, :DN_DV] - _dot1(uw[h][:, DN_DV:], Sh[h]) for h in hs]
    o = [_dot1(qc[h] * eG[h], Sh[h]) + _dot1(aqk[h], v_new[h]) for h in hs]
    for h in hs:
        Gl = Gc[h][C - 1:C, :]
        kdec = k[h] * jnp.exp(Gl - Gc[h])
        S[h] = Sh[h] * jnp.exp(Gl) + _dot1(kdec.T, v_new[h])
        zs = slice(h * DN_DV, (h + 1) * DN_DV)
        o_ref[:, zs] = _dn_out(o[h], z_ref[:, zs], nw_ref).astype(o_ref.dtype)

    @pl.when(c == pl.num_programs(1) - 1)
    def _():
        s_out_ref[...] = S[...]


def _dn_gate_rows(a_log, dt_bias):
    pad = lambda v: jnp.zeros((1, 128), F32).at[0, DNA_COL:DNA_COL + DN_HEADS].set(v)
    return pad(a_log), pad(dt_bias)


def _dn_prompt(dnx, z, small, conv_w, a_log, dt_bias, norm_w, B, T):
    C = DN_CHUNK
    assert T % C == 0
    nc = T // C
    Wx = dnx.shape[1]
    Wz = z.shape[1]
    aT = small[:, DNA_COL:DNA_COL + DN_HEADS].reshape(B, T, DN_HEADS).transpose(0, 2, 1)
    al, dtb = _dn_gate_rows(a_log, dt_bias)
    full = lambda s: pl.BlockSpec(s, lambda b, c: (0,) * len(s))
    row = lambda w: pl.BlockSpec((C, w), lambda b, c: (b * nc + c, 0))
    return pl.pallas_call(
        _dn_prompt_body,
        out_shape=(jax.ShapeDtypeStruct((B * T, Wz), BF16),
                   jax.ShapeDtypeStruct((B, DN_HEADS, DN_DK, DN_DV), F32)),
        grid=(B, nc),
        in_specs=[row(Wx), row(Wz), row(128), pl.BlockSpec((None, DN_HEADS, C), lambda b, c: (b, 0, c)),
                  full((DN_CONV, Wx)), full((1, 128)), full((DN_HEADS, 1)), full((1, 128)), full((DN_HEADS, 1)),
                  full((1, DN_DV))],
        out_specs=(row(Wz), pl.BlockSpec((None, DN_HEADS, DN_DK, DN_DV), lambda b, c: (b, 0, 0, 0))),
        scratch_shapes=[pltpu.VMEM((C + 8, Wx), F32), pltpu.VMEM((DN_HEADS, DN_DK, DN_DV), F32)],
        compiler_params=_cparams(("parallel", "arbitrary"), 48),
        name="deltanet_prompt",
    )(dnx, z, small, aT, conv_w, al, a_log.reshape(DN_HEADS, 1), dtb, dt_bias.reshape(DN_HEADS, 1),
      norm_w.reshape(1, DN_DV))


def _dn_dec_body(x_ref, hist_ref, sm_ref, z_ref, s0_ref, cw_ref, al_ref, dtb_ref, nw_ref,
                 o_ref, conv_o, s_o):
    x = x_ref[...]
    hist = hist_ref[...]
    y = cw_ref[DN_CONV - 1:DN_CONV, :] * x
    for j in range(DN_CONV - 1):
        y = y + cw_ref[j:j + 1, :] * hist[j:j + 1, :]
    conv_o[0:DN_CONV - 2, :] = hist[1:DN_CONV - 1, :]
    conv_o[DN_CONV - 2:DN_CONV - 1, :] = x
    y = _silu(y)
    g, beta = _dn_gates(sm_ref[...], al_ref, dtb_ref)
    n = DN_DK
    diag = lax.broadcasted_iota(jnp.int32, (n, n), 0) == lax.broadcasted_iota(jnp.int32, (n, n), 1)

    def column(r):
        return jnp.sum(jnp.where(diag, jnp.broadcast_to(r, (n, n)), 0.0), axis=1, keepdims=True)

    nh = DN_HEADS
    scale = DN_DK ** -0.5
    for h in range(nh):
        q = _l2(y[:, h * DN_DK:(h + 1) * DN_DK])
        k = _l2(y[:, (nh + h) * DN_DK:(nh + h + 1) * DN_DK])
        v = y[:, 2 * nh * DN_DK + h * DN_DV:2 * nh * DN_DK + (h + 1) * DN_DV]
        eg = jnp.exp(g[:, DNA_COL + h:DNA_COL + h + 1])
        bt = beta[:, DNB_COL + h:DNB_COL + h + 1]
        kcol = column(k)
        Sh = s0_ref[h]
        kS = jnp.sum(Sh * kcol, axis=0, keepdims=True)
        qS = jnp.sum(Sh * column(q), axis=0, keepdims=True)
        v_new = bt * (v - eg * kS)
        o = scale * (eg * qS + jnp.sum(q * k, axis=-1, keepdims=True) * v_new)
        s_o[h] = Sh * eg + kcol * v_new
        o_ref[:, h * DN_DV:(h + 1) * DN_DV] = _dn_out(o, z_ref[:, h * DN_DV:(h + 1) * DN_DV], nw_ref).astype(o_ref.dtype)


def _dn_decode(dnx, hist, small, z, s0, conv_w, a_log, dt_bias, norm_w):
    DB, Wx = dnx.shape
    Wz = z.shape[1]
    al, dtb = _dn_gate_rows(a_log, dt_bias)
    full = lambda s: pl.BlockSpec(s, lambda b: (0,) * len(s))
    one = lambda w: pl.BlockSpec((None, 1, w), lambda b: (b, 0, 0))
    nh = DN_CONV - 1
    st = pl.BlockSpec((None, DN_HEADS, DN_DK, DN_DV), lambda b: (b, 0, 0, 0))
    return pl.pallas_call(
        _dn_dec_body,
        out_shape=(jax.ShapeDtypeStruct((DB, 1, Wz), BF16), jax.ShapeDtypeStruct((DB, nh, Wx), F32),
                   jax.ShapeDtypeStruct(s0.shape, F32)),
        grid=(DB,),
        in_specs=[one(Wx), pl.BlockSpec((None, nh, Wx), lambda b: (b, 0, 0)), one(128), one(Wz), st,
                  full((DN_CONV, Wx)), full((1, 128)), full((1, 128)), full((1, DN_DV))],
        out_specs=(one(Wz), pl.BlockSpec((None, nh, Wx), lambda b: (b, 0, 0)), st),
        compiler_params=_cparams(("parallel",), 40),
        name="deltanet_decode",
    )(dnx.reshape(DB, 1, Wx), hist, small.reshape(DB, 1, 128), z.reshape(DB, 1, Wz), s0, conv_w, al, dtb,
      norm_w.reshape(1, DN_DV))


def _top_ranks(x, rows, n_rows):
    rank = jnp.full(x.shape, float(PEER_TOPK), F32)
    vals = []
    for k in range(PEER_TOPK):
        m = jnp.max(x, axis=0, keepdims=True)
        first = jnp.min(jnp.where(x == m, rows, n_rows), axis=0, keepdims=True)
        hit = rows == first
        rank = jnp.where(hit, float(k), rank)
        x = jnp.where(hit, -jnp.inf, x)
        vals.append(m)
    return jnp.concatenate(vals, axis=0), rank


def _peer_pairs():
    K = PEER_TOPK
    pairs = [(a, b) for a in range(K) for b in range(K) if (a + 1) * (b + 1) <= K]
    n_pad = -(-len(pairs) // 16) * 16
    sel_a = np.zeros((n_pad, K), np.float32)
    sel_b = np.zeros((n_pad, K), np.float32)
    for r, (a, b) in enumerate(pairs):
        sel_a[r, a] = 1.0
        sel_b[r, b] = 1.0
    return len(pairs), sel_a, sel_b


def _peer_route_body(qh_ref, keys_ref, sa_ref, sb_ref, sat_ref, r2_o, e2_o, c1_o, e1z_o, *, n_pairs):
    tr = qh_ref.shape[0]
    K = PEER_TOPK
    half = PEER_DKEY // 2
    rows = lax.broadcasted_iota(jnp.int32, (PEER_NKEYS, tr), 0)
    n_cand = sa_ref.shape[0]
    crow = lax.broadcasted_iota(jnp.int32, (n_cand, tr), 0)
    sel_a, sel_b = sa_ref[...], sb_ref[...]
    for h in range(PEER_HEADS):
        s1 = _dot3(keys_ref[h, 0], qh_ref[:, h * PEER_DKEY:h * PEER_DKEY + half], 1, 1)
        s2 = _dot3(keys_ref[h, 1], qh_ref[:, h * PEER_DKEY + half:(h + 1) * PEER_DKEY], 1, 1)
        v1, r1 = _top_ranks(s1, rows, PEER_NKEYS)
        v2, r2 = _top_ranks(s2, rows, PEER_NKEYS)
        cand = _dot_sel_lhs(sel_a, v1) + _dot_sel_lhs(sel_b, v2)
        cand = jnp.where(crow < n_pairs, cand, -jnp.inf)
        wmat = _dot_sel_lhs(sel_a, jnp.exp(v1 - v1[0:1, :])) * _dot_sel_lhs(sel_b, jnp.exp(v2 - v2[0:1, :]))
        chosen = jnp.zeros(cand.shape, F32)
        for _ in range(K):
            m = jnp.max(cand, axis=0, keepdims=True)
            first = jnp.min(jnp.where(cand == m, crow, n_cand), axis=0, keepdims=True)
            hit = crow == first
            chosen = jnp.where(hit, 1.0, chosen)
            cand = jnp.where(hit, -jnp.inf, cand)
        z = jnp.sum(chosen * wmat, axis=0, keepdims=True)
        cnt = _dot1(sat_ref[...], chosen)
        c1 = jnp.zeros(s1.shape, F32)
        for a in range(K):
            c1 = jnp.where(r1 == float(a), cnt[a:a + 1, :], c1)
        r2_o[h] = r2.astype(r2_o.dtype)
        e2_o[h] = jnp.exp(s2 - v2[0:1, :]).astype(e2_o.dtype)
        c1_o[h] = c1
        e1z_o[h] = jnp.exp(s1 - v1[0:1, :]) / z


def _peer_route(qh, keys):
    N = qh.shape[0]
    tr = min(N, 256)
    n_pairs, sel_a, sel_b = _peer_pairs()
    spec = pl.BlockSpec((PEER_HEADS, PEER_NKEYS, tr), lambda i: (0, 0, i))
    shp = lambda dt: jax.ShapeDtypeStruct((PEER_HEADS, PEER_NKEYS, N), dt)
    full = lambda a: pl.BlockSpec(a.shape, lambda i: (0,) * a.ndim)
    consts = [jnp.asarray(sel_a, BF16), jnp.asarray(sel_b, BF16), jnp.asarray(sel_a.T, BF16)]
    return pl.pallas_call(
        functools.partial(_peer_route_body, n_pairs=n_pairs),
        out_shape=(shp(BF16), shp(BF16), shp(F32), shp(F32)),
        grid=(N // tr,),
        in_specs=[pl.BlockSpec((tr, qh.shape[1]), lambda i: (i, 0)), full(keys)] + [full(c) for c in consts],
        out_specs=(spec, spec, spec, spec),
        compiler_params=_cparams(("parallel",), 40),
        name="peer_route",
    )(qh, keys, *consts)


def _peer_dense_body(h_ref, u_ref, v_ref, r2_ref, e2_ref, c1_ref, e1z_ref, o_ref, a0, a1, x0, x1):
    jj = pl.program_id(1)
    te = a0.shape[0]
    groups = te // PEER_NKEYS
    n_first = c1_ref.shape[1]

    @pl.when(jj == 0)
    def _():
        o_ref[...] = jnp.zeros(o_ref.shape, F32)
        a1[...] = jnp.zeros(a1.shape, F32)
        x0[...] = jnp.zeros(x0.shape, x0.dtype)
        x1[...] = jnp.zeros(x1.shape, x1.dtype)

    def stage1(half):
        return _dg(u_ref[half * te:(half + 1) * te, :], h_ref[...], 1, 1)

    def stage2(tile, a_ref):
        parts = []
        for ii in range(groups):
            irow = jnp.clip(tile * groups + ii, 0, n_first - 1)
            w = jnp.zeros((PEER_NKEYS, h_ref.shape[0]), BF16)
            for h in range(PEER_HEADS):
                c1 = c1_ref[h, pl.ds(irow, 1), :].astype(BF16)
                e1 = e1z_ref[h, pl.ds(irow, 1), :].astype(BF16)
                w = w + jnp.where(r2_ref[h] < c1, e2_ref[h] * e1, jnp.zeros_like(w))
            parts.append(w.astype(F32) * _gelu_tanh(a_ref[ii * PEER_NKEYS:(ii + 1) * PEER_NKEYS, :]))
        return jnp.concatenate(parts, axis=0).T.astype(x0.dtype)

    def stage3(x_ref, half):
        return _dg(x_ref[...], v_ref[half * te:(half + 1) * te, :])

    a0[...] = stage1(0)
    x1_new = stage2(2 * jj - 1, a1)
    d0 = stage3(x0, 0)
    x1[...] = x1_new
    a1[...] = stage1(1)
    x0[...] = stage2(2 * jj, a0)
    o_ref[...] += d0 + stage3(x1, 1)


def _peer_dense(h2, u, v, r2, e2, c1, e1z, tm):
    N, D = h2.shape
    NE = u.shape[0]
    te = 256
    tm = min(tm, N)
    njj = NE // (2 * te)
    tok = lambda: pl.BlockSpec((PEER_HEADS, PEER_NKEYS, tm), lambda i, j: (0, 0, i))
    return pl.pallas_call(
        _peer_dense_body,
        out_shape=jax.ShapeDtypeStruct((N, D), F32),
        grid=(N // tm, njj + 1),
        in_specs=[pl.BlockSpec((tm, D), lambda i, j: (i, 0)),
                  pl.BlockSpec((2 * te, D), lambda i, j: (jnp.minimum(j, njj - 1), 0)),
                  pl.BlockSpec((2 * te, D), lambda i, j: (jnp.maximum(j - 1, 0), 0)),
                  tok(), tok(), tok(), tok()],
        out_specs=pl.BlockSpec((tm, D), lambda i, j: (i, 0)),
        scratch_shapes=[pltpu.VMEM((te, tm), F32), pltpu.VMEM((te, tm), F32),
                        pltpu.VMEM((tm, te), BF16), pltpu.VMEM((tm, te), BF16)],
        compiler_params=_cparams(("parallel", "arbitrary"), 56),
        name="peer_dense",
    )(h2, u, v, r2, e2, c1, e1z)


def _peer(h2, wq, keys, u, v, tm):
    qh = _matmul(h2, wq, tn=1024, name="peer_query")
    r2, e2, c1, e1z = _peer_route(qh, keys)
    return _peer_dense(h2, u, v, r2, e2, c1, e1z, tm)


def _in_proj_weights(w_in):
    D = w_in.shape[0]
    nq = NSA_HEADS * HEAD_DIM
    nkv = 6 * NSA_KV * HEAD_DIM
    ng = NSA_HEADS * 3
    ndn = DN_HEADS * (2 * DN_DK + DN_DV)
    nz = DN_HEADS * DN_DV
    o_gate = nq + nkv
    o_dn = o_gate + ng
    o_a = o_dn + ndn
    o_b = o_a + DN_HEADS
    o_z = o_b + DN_HEADS
    o_m = o_z + nz
    small = jnp.concatenate([w_in[:, o_gate:o_dn], w_in[:, o_a:o_z],
                             jnp.zeros((D, 128 - ng - 2 * DN_HEADS), w_in.dtype)], axis=1)
    cast = lambda a: a.astype(BF16)
    return (cast(w_in[:, :o_gate]), cast(small), cast(w_in[:, o_dn:o_a]), cast(w_in[:, o_z:o_m]),
            cast(w_in[:, o_m:]))


def _project(h, wts):
    w_qkv, w_small, w_dn, w_z, w_merge = wts
    return (_matmul(h, w_qkv, tn=512, name="proj_qkv"), _matmul(h, w_small, tn=128, name="proj_small"),
            _matmul(h, w_dn, tn=1024, name="proj_dn"), _matmul(h, w_z, tn=1024, name="proj_z"),
            _matmul(h, w_merge, tn=1024, name="proj_merge"))


def _gate_groups(small, M):
    return small[:, GATE_COL:GATE_COL + 3 * NSA_HEADS].reshape(M, NSA_KV, 3 * NSA_REP).transpose(1, 0, 2)


def kernel(x_prompt, x_sample, cache_cmp_k, cache_cmp_v, cache_slc_k, cache_slc_v, state_win_k, state_win_v,
           state_conv, state_delta, page_table, c_prompt, c_sample, w_ada, b_ada, norm1_w, norm2_w, w_in,
           q_norm_w, k_norm_w, w_cmp_k, b_cmp_k, w_cmp_v, b_cmp_v, dn_conv_w, dn_A_log, dn_dt_bias, dn_norm_w,
           w_br_a, w_br_b, w_out, w_peer_q, w_peer_keys, w_peer_u, w_peer_v):
    depth = w_in.shape[0]
    B, T, D = x_prompt.shape
    DB = x_sample.shape[0]
    assert x_sample.shape[1] == 1
    NP = B * T
    page = cache_cmp_k.shape[2]
    past_len = page_table.shape[1] * page
    gw = NSA_KV * HEAD_DIM

    yp = x_prompt.reshape(NP, D)
    ys = x_sample.reshape(DB, D)
    p_hist, s_hist = [], []
    cos_p, sin_p = _rope_tables(np.arange(T))
    cos_s, sin_s = _rope_tables(np.full((DB,), past_len))
    n_cp = T // CMP_STRIDE
    cos_cp, sin_cp = _rope_tables(np.arange(n_cp) * CMP_STRIDE + CMP_LEN - 1)
    n_cs = past_len // CMP_STRIDE
    cos_cs, sin_cs = _rope_tables(np.arange(n_cs) * CMP_STRIDE + CMP_LEN - 1)
    n_ada = -(-(B + DB) // 16) * 16
    tile2 = lambda a: jnp.concatenate([a] * NSA_KV, axis=-1)

    for l in range(depth):
        c_all = jnp.concatenate([c_prompt, c_sample, jnp.zeros((n_ada - B - DB, D), F32)], axis=0)
        ada = _matmul(c_all, w_ada[l], tn=512, bias=b_ada[l], name="adaln")
        mods = [ada[:, i * D:(i + 1) * D] for i in range(6)]
        mp = [m[:B] for m in mods]
        ms = [m[B:B + DB] for m in mods]

        wts = _in_proj_weights(w_in[l])
        wa, wb, wo = w_br_a[l].astype(BF16), w_br_b[l].astype(BF16), w_out[l].astype(BF16)
        wq = w_peer_q[l].transpose(1, 0, 2).reshape(D, PEER_HEADS * PEER_DKEY).astype(BF16)
        pu, pv = w_peer_u[l].astype(BF16), w_peer_v[l].astype(BF16)
        wck, wcv = tile2(w_cmp_k[l]), tile2(w_cmp_v[l])
        bck, bcv = tile2(b_cmp_k[l]).reshape(1, gw), tile2(b_cmp_v[l]).reshape(1, gw)

        hp = _normmod(yp, norm1_w[l], mp[1], mp[0], T)
        qkv, small, dnx, z, mg = _project(hp, wts)
        q, kc, vc, ks, vs, kw, vw, ksb, vsb, kwb, vwb = _qkprep(qkv, cos_p, sin_p, q_norm_w[l], k_norm_w[l], T)
        ck = _compress_prompt(kc.reshape(B, T, gw), wck, bck, cos_cp, sin_cp, True)
        cv = _compress_prompt(vc.reshape(B, T, gw), wcv, bcv, cos_cp, sin_cp, False)
        o_nsa = _nsa_prompt(q, _gate_groups(small, NP), ck, cv, ksb, vsb, kwb, vwb, B, T)
        o_dn, p_delta = _dn_prompt(dnx, z, small, dn_conv_w[l], dn_A_log[l], dn_dt_bias[l], dn_norm_w[l], B, T)
        mixed = _merge(o_nsa, o_dn, mg, wa, wb)
        x1p, h2p = _outproj(yp, mixed, wo, mp[2], norm2_w[l], mp[4], mp[3], T)

        keep = min(WINDOW, T)
        r5 = lambda a: a.reshape(B, T, NSA_KV, HEAD_DIM)
        p_conv = dnx.reshape(B, T, -1)[:, T - (DN_CONV - 1):]
        p_hist.append((r5(kc), r5(vc), r5(ks), r5(vs), r5(kw)[:, T - keep:], r5(vw)[:, T - keep:], p_conv, p_delta))

        hs = _normmod(ys, norm1_w[l], ms[1], ms[0], 1)
        qkv, small, dnx, z, mg = _project(hs, wts)
        q, kc, vc, ks, vs, kw, vw, _, _, _, _ = _qkprep(qkv, cos_s, sin_s, q_norm_w[l], k_norm_w[l], 1)
        pool = cache_cmp_k.shape[1]
        rows = lambda c: c[l].reshape(c.shape[1], c.shape[2] * NSA_KV, HEAD_DIM)
        q8 = q.reshape(DB, NSA_HEADS, HEAD_DIM)
        ocmp, idx = _nsa_dec1(q8, page_table, rows(cache_cmp_k), rows(cache_cmp_v),
                              _interleaved_cmp_weights(w_cmp_k[l]), bck, _interleaved_cmp_weights(w_cmp_v[l]), bcv,
                              cos_cs, sin_cs, past_len)
        win = state_win_k.shape[2]
        gate4 = small[:, GATE_COL:GATE_COL + 3 * NSA_HEADS].reshape(DB, NSA_KV, NSA_REP, 3)
        r4 = lambda a: a.reshape(DB, NSA_KV, NSA_REP, HEAD_DIM)
        r3 = lambda a: a.reshape(DB, NSA_KV, HEAD_DIM)
        o8, s_win_k, s_win_v = _nsa_dec2(
            r4(q), gate4, r4(ocmp), page_table, idx[..., 0], rows(cache_slc_k), rows(cache_slc_v), r3(ks), r3(vs),
            rows(state_win_k), rows(state_win_v), r3(kw), r3(vw), past_len)
        o_dn, s_conv, s_delta = _dn_decode(dnx, state_conv[l], small, z, state_delta[l], dn_conv_w[l], dn_A_log[l],
                                           dn_dt_bias[l], dn_norm_w[l])
        mixed = _merge(o8.reshape(DB, NSA_HEADS * HEAD_DIM), o_dn.reshape(DB, -1), mg, wa, wb)
        x1s, h2s = _outproj(ys, mixed, wo, ms[2], norm2_w[l], ms[4], ms[3], 1)
        r5s = lambda a: a.reshape(DB, 1, NSA_KV, HEAD_DIM)
        s_hist.append((r5s(kc), r5s(vc), r5s(ks), r5s(vs), s_win_k.reshape(DB, win, NSA_KV, HEAD_DIM),
                       s_win_v.reshape(DB, win, NSA_KV, HEAD_DIM), s_conv, s_delta))

        peer_p = _peer(h2p, wq, w_peer_keys[l], pu, pv, 512)
        n_pad = -(-DB // 128) * 128
        h2s_pad = jnp.concatenate([h2s, jnp.zeros((n_pad - DB, D), h2s.dtype)], axis=0)
        peer_s = _peer(h2s_pad, wq, w_peer_keys[l], pu, pv, n_pad)[:DB]
        yp = _residual(x1p, peer_p, mp[5], T)
        ys = _residual(x1s, peer_s, ms[5], 1)

    stack = lambda hist: [jnp.stack(a) for a in zip(*hist)]
    return (yp.reshape(B, T, D), ys.reshape(DB, 1, D), *stack(p_hist), *stack(s_hist))
```
